```python
import jax
import jax.numpy as jnp
from jax import lax
import numpy as np

D_MODEL = 4096
BATCH = 1
SEQ = 8192
DEPTH = 1

PLE_DIM = 256
EPS = 1e-6
NEG_INF = -1e30

MLA_HEADS = 16
Q_LORA = 1024
KV_LORA = 512
QK_NOPE = 128
QK_ROPE = 64
V_HEAD = 128
MLA_WIDTH = MLA_HEADS * V_HEAD
ROPE_THETA = 10000.0
Q_BLOCK = 128

DIL_HEADS = 16
DIL_HEAD_DIM = 128
DIL_WIDTH = DIL_HEADS * DIL_HEAD_DIM
DIL_PATTERNS = ((128, 1), (512, 4), (2048, 16))
ALIBI_MAX_BIAS = 8.0

MIX_WIDTH = MLA_WIDTH + DIL_WIDTH
IN_SPLITS = (Q_LORA, KV_LORA, QK_ROPE, MLA_WIDTH, DIL_WIDTH, DIL_WIDTH, DIL_WIDTH, DIL_WIDTH)
IN_WIDTH = sum(IN_SPLITS)

kernel_name = 'hybrid_mla_dilated_encoder_layer'


def rms_norm(x, g):
    xf = x.astype(jnp.float32)
    y = xf * lax.rsqrt(jnp.mean(xf * xf, axis=-1, keepdims=True) + EPS)
    return (y * g.astype(jnp.float32)).astype(x.dtype)


def apply_rope(t, positions):
    half = t.shape[-1] // 2
    inv = ROPE_THETA ** (-jnp.arange(half, dtype=jnp.float32) / half)
    ang = positions.astype(jnp.float32)[:, :, None] * inv
    ang = ang.reshape(ang.shape[:2] + (1,) * (t.ndim - 3) + (half,))
    cos, sin = jnp.cos(ang), jnp.sin(ang)
    t1 = t[..., :half].astype(jnp.float32)
    t2 = t[..., half:].astype(jnp.float32)
    return jnp.concatenate([t1 * cos - t2 * sin, t1 * sin + t2 * cos], axis=-1).astype(t.dtype)


def mla_attention(q, k, v):
    B, S, H, Dq = q.shape
    nb = S // Q_BLOCK
    scale = Dq ** -0.5
    qb = q.reshape(B, nb, Q_BLOCK, H, Dq).transpose(1, 0, 2, 3, 4)

    def one_block(qi):
        s = jnp.einsum('bqhd,bkhd->bhqk', qi, k, preferred_element_type=jnp.float32) * scale
        pr = jax.nn.softmax(s, axis=-1)
        return jnp.einsum('bhqk,bkhd->bqhd', pr.astype(v.dtype), v)

    out = lax.map(one_block, qb)
    return out.transpose(1, 0, 2, 3, 4).reshape(B, S, H, v.shape[-1])


def dilated_pattern_attention(q, k, v, slopes, window, dilation):
    B, S, H, D = q.shape
    half = window // (2 * dilation)
    L = S // dilation
    nb = -(-L // half)
    Lp = nb * half

    def to_classes(t):
        t = t.reshape(B, L, dilation, H, D).transpose(0, 2, 1, 3, 4)
        return jnp.pad(t, ((0, 0), (0, 0), (0, Lp - L), (0, 0), (0, 0)))

    def band(t):
        tp = jnp.pad(t, ((0, 0), (0, 0), (half, half), (0, 0), (0, 0)))
        tb = tp.reshape(B, dilation, nb + 2, half, H, D)
        return jnp.concatenate([tb[:, :, :-2], tb[:, :, 1:-1], tb[:, :, 2:]], axis=3)

    qb = to_classes(q).reshape(B, dilation, nb, half, H, D)
    kb = band(to_classes(k))
    vb = band(to_classes(v))

    qj = jnp.arange(Lp).reshape(nb, half)
    kj = jnp.arange(nb)[:, None] * half - half + jnp.arange(3 * half)[None, :]
    delta = jnp.abs(qj[:, :, None] - kj[:, None, :])
    valid = (delta <= half) & (kj[:, None, :] >= 0) & (kj[:, None, :] < L)
    alibi = slopes[None, :, None, None] * (dilation * delta).astype(jnp.float32)[:, None]

    s = jnp.einsum('bcnqhd,bcnkhd->bcnhqk', qb, kb, preferred_element_type=jnp.float32)
    s = s * (DIL_HEAD_DIM ** -0.5) - alibi
    s = jnp.where(valid[:, None], s, NEG_INF)
    m = jnp.max(s, axis=-1, keepdims=True)
    e = jnp.exp(s - m)
    l = jnp.sum(e, axis=-1, keepdims=True)
    o = jnp.einsum('bcnhqk,bcnkhd->bcnqhd', e / l, vb.astype(jnp.float32))
    lse = (m + jnp.log(l))[..., 0]

    o = o.reshape(B, dilation, Lp, H, D)[:, :, :L].transpose(0, 2, 1, 3, 4).reshape(B, S, H, D)
    lse = lse.transpose(0, 1, 2, 4, 3).reshape(B, dilation, Lp, H)[:, :, :L]
    lse = lse.transpose(0, 2, 1, 3).reshape(B, S, H)
    return o, lse


def dilated_mixture_attention(q, k, v):
    H = q.shape[2]
    slopes = jnp.asarray(2.0 ** (-ALIBI_MAX_BIAS * np.arange(1, H + 1) / H), jnp.float32)
    results = [dilated_pattern_attention(q, k, v, slopes, w, d) for w, d in DIL_PATTERNS]
    outs = jnp.stack([r[0] for r in results], axis=0)
    lses = jnp.stack([r[1] for r in results], axis=0)
    weights = jax.nn.softmax(lses, axis=0)
    return jnp.sum(weights[..., None] * outs, axis=0)


def setup_inputs(seed: int = 0) -> dict:
    key = jax.random.key(seed)
    ks = jax.random.split(key, 16)

    def dense(k, shape):
        return jax.random.normal(k, shape, jnp.float32) * shape[-2] ** -0.5

    def gain(k, shape):
        return 1.0 + 0.1 * jax.random.normal(k, shape, jnp.float32)

    return {
        'x': jax.random.normal(ks[0], (BATCH, SEQ, D_MODEL), jnp.float32),
        'p': jax.random.normal(ks[1], (DEPTH, BATCH, SEQ, PLE_DIM), jnp.float32),
        'positions': jnp.broadcast_to(jnp.arange(SEQ, dtype=jnp.int32)[None, :], (BATCH, SEQ)),
        'g_mix': gain(ks[2], (DEPTH, D_MODEL)),
        'w_in': dense(ks[3], (DEPTH, D_MODEL, IN_WIDTH)),
        'g_q_latent': gain(ks[4], (DEPTH, Q_LORA)),
        'w_uq': dense(ks[5], (DEPTH, Q_LORA, MLA_HEADS * (QK_NOPE + QK_ROPE))),
        'g_kv_latent': gain(ks[6], (DEPTH, KV_LORA)),
        'w_ukv': dense(ks[7], (DEPTH, KV_LORA, MLA_HEADS * (QK_NOPE + V_HEAD))),
        'g_out_mla': gain(ks[8], (DEPTH, MLA_WIDTH)),
        'g_out_dil': gain(ks[9], (DEPTH, DIL_WIDTH)),
        'w_out': dense(ks[10], (DEPTH, MIX_WIDTH, D_MODEL)),
        'w_ple': dense(ks[11], (DEPTH, PLE_DIM, D_MODEL)),
        'g_ple': gain(ks[12], (DEPTH, D_MODEL)),
        'w_ple_gate': dense(ks[13], (DEPTH, D_MODEL, D_MODEL)),
        'g_final': gain(ks[14], (D_MODEL,)),
    }


def reference(x, p, positions, g_mix, w_in, g_q_latent, w_uq, g_kv_latent, w_ukv,
              g_out_mla, g_out_dil, w_out, w_ple, g_ple, w_ple_gate, g_final):
    B, S, _ = x.shape
    split_idx = [int(v) for v in np.cumsum(IN_SPLITS)[:-1]]
    for i in range(DEPTH):
        h = rms_norm(x, g_mix[i])
        proj = jnp.einsum('bsd,de->bse', h, w_in[i])
        c_q, c_kv, k_rope, gate_a, q_b, k_b, v_b, gate_b = jnp.split(proj, split_idx, axis=-1)

        q_a = jnp.einsum('bsr,re->bse', rms_norm(c_q, g_q_latent[i]), w_uq[i])
        q_a = q_a.reshape(B, S, MLA_HEADS, QK_NOPE + QK_ROPE)
        q_nope, q_pe = q_a[..., :QK_NOPE], apply_rope(q_a[..., QK_NOPE:], positions)
        kv = jnp.einsum('bsr,re->bse', rms_norm(c_kv, g_kv_latent[i]), w_ukv[i])
        kv = kv.reshape(B, S, MLA_HEADS, QK_NOPE + V_HEAD)
        k_nope, v_a = kv[..., :QK_NOPE], kv[..., QK_NOPE:]
        k_pe = jnp.broadcast_to(apply_rope(k_rope, positions)[:, :, None, :], (B, S, MLA_HEADS, QK_ROPE))
        y_a = mla_attention(jnp.concatenate([q_nope, q_pe], axis=-1),
                            jnp.concatenate([k_nope, k_pe], axis=-1), v_a)
        y_a = rms_norm(y_a.reshape(B, S, MLA_WIDTH), g_out_mla[i]) * jax.nn.silu(gate_a)

        shp = (B, S, DIL_HEADS, DIL_HEAD_DIM)
        y_b = dilated_mixture_attention(q_b.reshape(shp), k_b.reshape(shp), v_b.reshape(shp))
        y_b = rms_norm(y_b.reshape(B, S, DIL_WIDTH).astype(x.dtype), g_out_dil[i]) * jax.nn.silu(gate_b)

        x = x + jnp.einsum('bse,ed->bsd', jnp.concatenate([y_a, y_b], axis=-1), w_out[i])

        ple = jnp.einsum('bsc,cd->bsd', p[i], w_ple[i])
        gate = jax.nn.sigmoid(jnp.einsum('bsd,de->bse', rms_norm(x, g_ple[i]), w_ple_gate[i]))
        x = x + ple * gate
    return rms_norm(x, g_final)
```

```python
import functools

import numpy as np
import jax
import jax.numpy as jnp
from jax import lax
from jax.experimental import pallas as pl
from jax.experimental.pallas import tpu as pltpu

F32 = jnp.float32
BF16 = jnp.bfloat16

EPS = 1e-6
ROPE_THETA = 10000.0
ALIBI_MAX_BIAS = 8.0

MLA_HEADS = 16
Q_LORA = 1024
KV_LORA = 512
QK_NOPE = 128
QK_ROPE = 64
V_HEAD = 128
MLA_WIDTH = MLA_HEADS * V_HEAD
DIL_HEADS = 16
DIL_HEAD_DIM = 128
DIL_WIDTH = DIL_HEADS * DIL_HEAD_DIM
DIL_PATTERNS = ((128, 1), (512, 4), (2048, 16))

LANES = 128
QK_PAD = 256
ROPE_HALF = QK_ROPE // 2

COL_GATE_A = 0
COL_QB = COL_GATE_A + MLA_WIDTH
COL_KB = COL_QB + DIL_WIDTH
COL_VB = COL_KB + DIL_WIDTH
COL_GATE_B = COL_VB + DIL_WIDTH
COL_CQ = COL_GATE_B + DIL_WIDTH
COL_CKV = COL_CQ + Q_LORA
COL_KR = COL_CKV + KV_LORA
PROJ_WIDTH = 12288

M_INIT = -1e30
MASK_BIAS = -2e30

ROW_CHUNK = 128

VMEM_LIMIT = 56 * 1024 * 1024


def _params(*sem):
    return pltpu.CompilerParams(dimension_semantics=sem, vmem_limit_bytes=VMEM_LIMIT)


def _rms(x, g):
    return x * lax.rsqrt(jnp.mean(x * x, axis=-1, keepdims=True) + EPS) * g


def _rms_rows(x_ref, g_ref, h_ref):
    for r in range(0, x_ref.shape[0], ROW_CHUNK):
        rows = slice(r, r + ROW_CHUNK)
        h_ref[rows, :] = _rms(x_ref[rows, :].astype(F32), g_ref[...]).astype(h_ref.dtype)


def _in_proj_kernel(x_ref, g_ref, w_ref, o_ref, h_ref):
    @pl.when(pl.program_id(1) == 0)
    def _():
        _rms_rows(x_ref, g_ref, h_ref)

    o_ref[...] = jnp.dot(h_ref[...], w_ref[...], preferred_element_type=F32).astype(o_ref.dtype)


def _in_proj(x, g, w, bm=512, bn=1024):
    s, d = x.shape
    n = w.shape[1]
    return pl.pallas_call(
        _in_proj_kernel,
        grid=(s // bm, n // bn),
        in_specs=[
            pl.BlockSpec((bm, d), lambda i, j: (i, 0)),
            pl.BlockSpec((1, d), lambda i, j: (0, 0)),
            pl.BlockSpec((d, bn), lambda i, j: (0, j)),
        ],
        out_specs=pl.BlockSpec((bm, bn), lambda i, j: (i, j)),
        out_shape=jax.ShapeDtypeStruct((s, n), BF16),
        scratch_shapes=[pltpu.VMEM((bm, d), BF16)],
        compiler_params=_params("parallel", "arbitrary"),
        name="in_proj",
    )(x, g, w)


def _rope(t, cos, sin):
    return t * cos + pltpu.roll(t, 2 * ROPE_HALF, axis=1) * sin


def _mla_prep_kernel(cq_ref, ckv_ref, kr_ref, gq_ref, gkv_ref, cos_ref, sin_ref, wq_ref, wkv_ref,
                     q_out, k_out, v_out, cqn_ref, ckvn_ref, kpe_ref, *, heads):
    @pl.when(pl.program_id(1) == 0)
    def _():
        cqn_ref[...] = _rms(cq_ref[...].astype(F32), gq_ref[...]).astype(BF16)
        ckvn_ref[...] = _rms(ckv_ref[...].astype(F32), gkv_ref[...]).astype(BF16)
        kpe_ref[...] = _rope(kr_ref[...].astype(F32), cos_ref[...], sin_ref[...]).astype(BF16)

    q = jnp.dot(cqn_ref[...], wq_ref[...], preferred_element_type=F32)
    kv = jnp.dot(ckvn_ref[...], wkv_ref[...], preferred_element_type=F32)
    cos = cos_ref[...]
    sin = sin_ref[...]
    for g in range(heads):
        lo = g * QK_PAD
        mid = lo + LANES
        hi = lo + QK_PAD
        q_out[:, lo:mid] = q[:, lo:mid].astype(BF16)
        q_out[:, mid:hi] = _rope(q[:, mid:hi], cos, sin).astype(BF16)
        k_out[:, lo:mid] = kv[:, lo:mid].astype(BF16)
        k_out[:, mid:hi] = kpe_ref[...]
        v_out[:, g * V_HEAD:(g + 1) * V_HEAD] = kv[:, mid:hi].astype(BF16)


def _mla_prep(proj, gq, gkv, cos, sin, wq, wkv, bm=512, heads=4):
    s = proj.shape[0]
    bq = heads * QK_PAD
    kern = functools.partial(_mla_prep_kernel, heads=heads)
    return pl.pallas_call(
        kern,
        grid=(s // bm, MLA_HEADS // heads),
        in_specs=[
            pl.BlockSpec((bm, Q_LORA), lambda i, j: (i, COL_CQ // Q_LORA)),
            pl.BlockSpec((bm, KV_LORA), lambda i, j: (i, COL_CKV // KV_LORA)),
            pl.BlockSpec((bm, LANES), lambda i, j: (i, COL_KR // LANES)),
            pl.BlockSpec((1, Q_LORA), lambda i, j: (0, 0)),
            pl.BlockSpec((1, KV_LORA), lambda i, j: (0, 0)),
            pl.BlockSpec((bm, LANES), lambda i, j: (i, 0)),
            pl.BlockSpec((bm, LANES), lambda i, j: (i, 0)),
            pl.BlockSpec((Q_LORA, bq), lambda i, j: (0, j)),
            pl.BlockSpec((KV_LORA, bq), lambda i, j: (0, j)),
        ],
        out_specs=[
            pl.BlockSpec((bm, bq), lambda i, j: (i, j)),
            pl.BlockSpec((bm, bq), lambda i, j: (i, j)),
            pl.BlockSpec((bm, heads * V_HEAD), lambda i, j: (i, j)),
        ],
        out_shape=[
            jax.ShapeDtypeStruct((s, MLA_HEADS * QK_PAD), BF16),
            jax.ShapeDtypeStruct((s, MLA_HEADS * QK_PAD), BF16),
            jax.ShapeDtypeStruct((s, MLA_WIDTH), BF16),
        ],
        scratch_shapes=[
            pltpu.VMEM((bm, Q_LORA), BF16),
            pltpu.VMEM((bm, KV_LORA), BF16),
            pltpu.VMEM((bm, LANES), BF16),
        ],
        compiler_params=_params("parallel", "arbitrary"),
        name="mla_prep",
    )(proj, proj, proj, gq, gkv, cos, sin, wq, wkv)


def _softmax_step(s, v, m_ref, l_ref, acc_ref):
    m_prev = m_ref[...]
    m_new = jnp.maximum(m_prev, jnp.max(s, axis=-1, keepdims=True))
    alpha = jnp.exp(m_prev - m_new)
    p = jnp.exp(s - m_new)
    l_ref[...] = alpha * l_ref[...] + jnp.sum(p, axis=-1, keepdims=True)
    acc_ref[...] = alpha * acc_ref[...] + jnp.dot(p.astype(BF16), v, preferred_element_type=F32)
    m_ref[...] = m_new


def _softmax_init(m_ref, l_ref, acc_ref):
    m_ref[...] = jnp.full(m_ref.shape, M_INIT, F32)
    l_ref[...] = jnp.zeros(l_ref.shape, F32)
    acc_ref[...] = jnp.zeros(acc_ref.shape, F32)


_NT = (((1,), (1,)), ((), ()))


def _mla_attn_kernel(q_ref, k_ref, v_ref, o_ref, m_ref, l_ref, acc_ref, *, tk):
    _softmax_init(m_ref, l_ref, acc_ref)
    q = q_ref[...]

    def body(c, carry):
        ks = pl.multiple_of(c * tk, tk)
        s = lax.dot_general(q, k_ref[pl.ds(ks, tk), :], _NT, preferred_element_type=F32)
        _softmax_step(s, v_ref[pl.ds(ks, tk), :], m_ref, l_ref, acc_ref)
        return carry

    lax.fori_loop(0, k_ref.shape[0] // tk, body, 0)
    o_ref[...] = (acc_ref[...] / l_ref[...]).astype(o_ref.dtype)


def _mla_attn(q, k, v, tq=512, tk=512):
    s = q.shape[0]
    kern = functools.partial(_mla_attn_kernel, tk=tk)
    return pl.pallas_call(
        kern,
        grid=(MLA_HEADS, s // tq),
        in_specs=[
            pl.BlockSpec((tq, QK_PAD), lambda h, i: (i, h)),
            pl.BlockSpec((s, QK_PAD), lambda h, i: (0, h)),
            pl.BlockSpec((s, V_HEAD), lambda h, i: (0, h)),
        ],
        out_specs=pl.BlockSpec((tq, V_HEAD), lambda h, i: (i, h)),
        out_shape=jax.ShapeDtypeStruct((s, MLA_WIDTH), BF16),
        scratch_shapes=[
            pltpu.VMEM((tq, 1), F32),
            pltpu.VMEM((tq, 1), F32),
            pltpu.VMEM((tq, V_HEAD), F32),
        ],
        compiler_params=_params("parallel", "arbitrary"),
        name="mla_attn",
    )(q, k, v)


DIL_T = 512
DIL_REACH = max(w // 2 for w, _ in DIL_PATTERNS)
DIL_SIDE = DIL_REACH // DIL_T
DIL_NCHUNK = 2 * DIL_SIDE + 1


def _dil_bias(r, slope):
    row = lax.broadcasted_iota(jnp.int32, (DIL_T, DIL_T), 0)
    col = lax.broadcasted_iota(jnp.int32, (DIL_T, DIL_T), 1)
    d = (r - DIL_SIDE) * DIL_T + col - row
    ad = jnp.abs(d)
    cnt = jnp.zeros((DIL_T, DIL_T), F32)
    for window, dil in DIL_PATTERNS:
        member = ((d & (dil - 1)) == 0) & (ad <= window // 2)
        cnt = cnt + member.astype(F32)
    bias = jnp.log(jnp.maximum(cnt, 1.0)) - slope * ad.astype(F32)
    return jnp.where(cnt > 0.0, bias, MASK_BIAS)


def _dil_attn_kernel(slope_ref, q_ref, k_ref, v_ref, o_ref, b_ref, m_ref, l_ref, acc_ref, *, nblk):
    h = pl.program_id(0)
    i = pl.program_id(1)

    @pl.when(i == 0)
    def _():
        slope = slope_ref[h]
        for r in range(DIL_NCHUNK):
            b_ref[r] = _dil_bias(r, slope)

    _softmax_init(m_ref, l_ref, acc_ref)
    q = q_ref[...]
    for r in range(DIL_NCHUNK):
        kc = i + (r - DIL_SIDE)

        @pl.when((kc >= 0) & (kc < nblk))
        def _():
            ks = pl.multiple_of(kc * DIL_T, DIL_T)
            s = lax.dot_general(q, k_ref[pl.ds(ks, DIL_T), :], _NT, preferred_element_type=F32)
            _softmax_step(s + b_ref[r], v_ref[pl.ds(ks, DIL_T), :], m_ref, l_ref, acc_ref)

    o_ref[...] = (acc_ref[...] / l_ref[...]).astype(o_ref.dtype)


def _dil_attn(proj, slopes):
    s = proj.shape[0]
    nblk = s // DIL_T
    kern = functools.partial(_dil_attn_kernel, nblk=nblk)
    hd = DIL_HEAD_DIM
    return pl.pallas_call(
        kern,
        grid=(DIL_HEADS, nblk),
        in_specs=[
            pl.BlockSpec(memory_space=pltpu.SMEM),
            pl.BlockSpec((DIL_T, hd), lambda h, i: (i, COL_QB // hd + h)),
            pl.BlockSpec((s, hd), lambda h, i: (0, COL_KB // hd + h)),
            pl.BlockSpec((s, hd), lambda h, i: (0, COL_VB // hd + h)),
        ],
        out_specs=pl.BlockSpec((DIL_T, hd), lambda h, i: (i, h)),
        out_shape=jax.ShapeDtypeStruct((s, DIL_WIDTH), BF16),
        scratch_shapes=[
            pltpu.VMEM((DIL_NCHUNK, DIL_T, DIL_T), F32),
            pltpu.VMEM((DIL_T, 1), F32),
            pltpu.VMEM((DIL_T, 1), F32),
            pltpu.VMEM((DIL_T, hd), F32),
        ],
        compiler_params=_params("arbitrary", "arbitrary"),
        name="dil_attn",
    )(slopes, proj, proj, proj)


def _gated_norm(a_ref, gate_ref, g_ref, y_ref):
    for r in range(0, a_ref.shape[0], ROW_CHUNK):
        rows = slice(r, r + ROW_CHUNK)
        gate = gate_ref[rows, :].astype(F32)
        y = _rms(a_ref[rows, :].astype(F32), g_ref[...]) * (gate * jax.nn.sigmoid(gate))
        y_ref[rows, :] = y.astype(y_ref.dtype)


def _out_proj_kernel(aa_ref, ab_ref, ga_ref, gb_ref, gma_ref, gmb_ref, x_ref, w_ref, o_ref, y_ref):
    @pl.when(pl.program_id(1) == 0)
    def _():
        _gated_norm(aa_ref, ga_ref, gma_ref, y_ref.at[:, :MLA_WIDTH])
        _gated_norm(ab_ref, gb_ref, gmb_ref, y_ref.at[:, MLA_WIDTH:])

    o_ref[...] = x_ref[...] + jnp.dot(y_ref[...], w_ref[...], preferred_element_type=F32)


def _out_proj(attn_a, attn_b, proj, g_a, g_b, x, w, bm=512, bn=512):
    s, d = x.shape
    mix = MLA_WIDTH + DIL_WIDTH
    return pl.pallas_call(
        _out_proj_kernel,
        grid=(s // bm, d // bn),
        in_specs=[
            pl.BlockSpec((bm, MLA_WIDTH), lambda i, j: (i, 0)),
            pl.BlockSpec((bm, DIL_WIDTH), lambda i, j: (i, 0)),
            pl.BlockSpec((bm, MLA_WIDTH), lambda i, j: (i, COL_GATE_A // MLA_WIDTH)),
            pl.BlockSpec((bm, DIL_WIDTH), lambda i, j: (i, COL_GATE_B // DIL_WIDTH)),
            pl.BlockSpec((1, MLA_WIDTH), lambda i, j: (0, 0)),
            pl.BlockSpec((1, DIL_WIDTH), lambda i, j: (0, 0)),
            pl.BlockSpec((bm, bn), lambda i, j: (i, j)),
            pl.BlockSpec((mix, bn), lambda i, j: (0, j)),
        ],
        out_specs=pl.BlockSpec((bm, bn), lambda i, j: (i, j)),
        out_shape=jax.ShapeDtypeStruct((s, d), F32),
        scratch_shapes=[pltpu.VMEM((bm, mix), BF16)],
        compiler_params=_params("parallel", "arbitrary"),
        name="out_proj",
    )(attn_a, attn_b, proj, proj, g_a, g_b, x, w)


def _ple_gate_kernel(xrow_ref, g_ref, wg_ref, p_ref, wp_ref, xt_ref, o_ref, h_ref):
    @pl.when(pl.program_id(1) == 0)
    def _():
        _rms_rows(xrow_ref, g_ref, h_ref)

    gate = jax.nn.sigmoid(jnp.dot(h_ref[...], wg_ref[...], preferred_element_type=F32))
    ple = jnp.dot(p_ref[...].astype(BF16), wp_ref[...], preferred_element_type=F32)
    o_ref[...] = xt_ref[...] + ple * gate


def _ple_gate(x, g, wg, p, wp, bm=512, bn=512):
    s, d = x.shape
    c = p.shape[1]
    return pl.pallas_call(
        _ple_gate_kernel,
        grid=(s // bm, d // bn),
        in_specs=[
            pl.BlockSpec((bm, d), lambda i, j: (i, 0)),
            pl.BlockSpec((1, d), lambda i, j: (0, 0)),
            pl.BlockSpec((d, bn), lambda i, j: (0, j)),
            pl.BlockSpec((bm, c), lambda i, j: (i, 0)),
            pl.BlockSpec((c, bn), lambda i, j: (0, j)),
            pl.BlockSpec((bm, bn), lambda i, j: (i, j)),
        ],
        out_specs=pl.BlockSpec((bm, bn), lambda i, j: (i, j)),
        out_shape=jax.ShapeDtypeStruct((s, d), F32),
        scratch_shapes=[pltpu.VMEM((bm, d), BF16)],
        compiler_params=_params("parallel", "arbitrary"),
        name="ple_gate",
    )(x, g, wg, p, wp, x)


def _final_norm_kernel(x_ref, g_ref, o_ref):
    o_ref[...] = _rms(x_ref[...], g_ref[...])


def _final_norm(x, g, bm=256):
    s, d = x.shape
    return pl.pallas_call(
        _final_norm_kernel,
        grid=(s // bm,),
        in_specs=[pl.BlockSpec((bm, d), lambda i: (i, 0)), pl.BlockSpec((1, d), lambda i: (0, 0))],
        out_specs=pl.BlockSpec((bm, d), lambda i: (i, 0)),
        out_shape=jax.ShapeDtypeStruct((s, d), F32),
        compiler_params=_params("parallel"),
        name="final_norm",
    )(x, g)


def _rope_dup(w):
    t1, t2 = w[..., :ROPE_HALF], w[..., ROPE_HALF:]
    return jnp.concatenate([t1, t2, t2, t1], axis=-1)


def _prep_w_in(w):
    d = w.shape[0]
    offs = np.cumsum((0, Q_LORA, KV_LORA, QK_ROPE, MLA_WIDTH, DIL_WIDTH, DIL_WIDTH, DIL_WIDTH, DIL_WIDTH))
    c_q, c_kv, k_r, gate_a, q_b, k_b, v_b, gate_b = [w[:, offs[n]:offs[n + 1]] for n in range(8)]
    q_b = q_b * (DIL_HEAD_DIM ** -0.5)
    pad = jnp.zeros((d, PROJ_WIDTH - COL_KR - LANES), w.dtype)
    return jnp.concatenate([gate_a, q_b, k_b, v_b, gate_b, c_q, c_kv, _rope_dup(k_r), pad], axis=1).astype(BF16)


def _prep_w_uq(w):
    r = w.shape[0]
    w = w.reshape(r, MLA_HEADS, QK_NOPE + QK_ROPE) * ((QK_NOPE + QK_ROPE) ** -0.5)
    w = jnp.concatenate([w[..., :QK_NOPE], _rope_dup(w[..., QK_NOPE:])], axis=-1)
    return w.reshape(r, MLA_HEADS * QK_PAD).astype(BF16)


def _rope_tables(positions):
    inv = ROPE_THETA ** (-jnp.arange(ROPE_HALF, dtype=F32) / ROPE_HALF)
    ang = positions.astype(F32)[:, None] * inv
    cos, sin = jnp.cos(ang), jnp.sin(ang)
    zero = jnp.zeros_like(cos)
    return (jnp.concatenate([cos, cos, zero, zero], axis=1),
            jnp.concatenate([-sin, sin, zero, zero], axis=1))


def kernel(x, p, positions, g_mix, w_in, g_q_latent, w_uq, g_kv_latent, w_ukv, g_out_mla, g_out_dil,
           w_out, w_ple, g_ple, w_ple_gate, g_final):
    b, s, d = x.shape
    assert b == 1 and s % DIL_T == 0 and s // DIL_T > DIL_SIDE
    slopes = jnp.asarray(2.0 ** (-ALIBI_MAX_BIAS * np.arange(1, DIL_HEADS + 1) / DIL_HEADS), F32)
    x2d = x.reshape(s, d)
    cos, sin = _rope_tables(positions[0])
    for i in range(g_mix.shape[0]):
        proj = _in_proj(x2d, g_mix[i][None], _prep_w_in(w_in[i]))
        q_a, k_a, v_a = _mla_prep(proj, g_q_latent[i][None], g_kv_latent[i][None], cos, sin,
                                  _prep_w_uq(w_uq[i]), w_ukv[i].astype(BF16))
        attn_a = _mla_attn(q_a, k_a, v_a)
        attn_b = _dil_attn(proj, slopes)
        x2d = _out_proj(attn_a, attn_b, proj, g_out_mla[i][None], g_out_dil[i][None], x2d,
                        w_out[i].astype(BF16))
        x2d = _ple_gate(x2d, g_ple[i][None], w_ple_gate[i].astype(BF16), p[i, 0], w_ple[i].astype(BF16))
    return _final_norm(x2d, g_final[None]).reshape(b, s, d)
```

```python
import functools

import numpy as np
import jax
import jax.numpy as jnp
from jax import lax
from jax.experimental import pallas as pl
from jax.experimental.pallas import tpu as pltpu

F32 = jnp.float32
BF16 = jnp.bfloat16

EPS = 1e-6
ROPE_THETA = 10000.0
ALIBI_MAX_BIAS = 8.0

MLA_HEADS = 16
Q_LORA = 1024
KV_LORA = 512
QK_NOPE = 128
QK_ROPE = 64
V_HEAD = 128
MLA_WIDTH = MLA_HEADS * V_HEAD
DIL_HEADS = 16
DIL_HEAD_DIM = 128
DIL_WIDTH = DIL_HEADS * DIL_HEAD_DIM
DIL_PATTERNS = ((128, 1), (512, 4), (2048, 16))

LANES = 128
QK_PAD = 256
V_PAD = 2 * LANES
ROPE_HALF = QK_ROPE // 2
LOG2E = float(np.log2(np.e))

COL_GATE_A = 0
COL_QB = COL_GATE_A + MLA_WIDTH
COL_KB = COL_QB + DIL_WIDTH
COL_VB = COL_KB + DIL_WIDTH
COL_GATE_B = COL_VB + DIL_WIDTH
COL_CQ = COL_GATE_B + DIL_WIDTH
COL_CKV = COL_CQ + Q_LORA
COL_KR = COL_CKV + KV_LORA
PROJ_WIDTH = 12288

MASK_BIAS = -1e30

ROW_CHUNK = 128

VMEM_LIMIT = 56 * 1024 * 1024


def _params(*sem):
    return pltpu.CompilerParams(dimension_semantics=sem, vmem_limit_bytes=VMEM_LIMIT)


def _rms(x, g):
    return x * lax.rsqrt(jnp.mean(x * x, axis=-1, keepdims=True) + EPS) * g


def _rms_rows(x_ref, g_ref, h_ref):
    for r in range(0, x_ref.shape[0], ROW_CHUNK):
        rows = slice(r, r + ROW_CHUNK)
        h_ref[rows, :] = _rms(x_ref[rows, :].astype(F32), g_ref[...]).astype(h_ref.dtype)


def _in_proj_kernel(x_ref, g_ref, w_ref, o_ref, h_ref):
    @pl.when(pl.program_id(1) == 0)
    def _():
        _rms_rows(x_ref, g_ref, h_ref)

    o_ref[...] = jnp.dot(h_ref[...], w_ref[...], preferred_element_type=F32).astype(o_ref.dtype)


def _in_proj(x, g, w, bm=512, bn=1024):
    s, d = x.shape
    n = w.shape[1]
    return pl.pallas_call(
        _in_proj_kernel,
        grid=(s // bm, n // bn),
        in_specs=[
            pl.BlockSpec((bm, d), lambda i, j: (i, 0)),
            pl.BlockSpec((1, d), lambda i, j: (0, 0)),
            pl.BlockSpec((d, bn), lambda i, j: (0, j)),
        ],
        out_specs=pl.BlockSpec((bm, bn), lambda i, j: (i, j)),
        out_shape=jax.ShapeDtypeStruct((s, n), BF16),
        scratch_shapes=[pltpu.VMEM((bm, d), BF16)],
        compiler_params=_params("parallel", "arbitrary"),
        name="in_proj",
    )(x, g, w)


def _rope(t, cos, sin):
    return t * cos + pltpu.roll(t, 2 * ROPE_HALF, axis=1) * sin


def _mla_prep_kernel(cq_ref, ckv_ref, kr_ref, gq_ref, gkv_ref, cos_ref, sin_ref, wq_ref, wkv_ref,
                     q_out, k_out, v_out, cqn_ref, ckvn_ref, kpe_ref, *, heads):
    @pl.when(pl.program_id(1) == 0)
    def _():
        cqn_ref[...] = _rms(cq_ref[...].astype(F32), gq_ref[...]).astype(BF16)
        ckvn_ref[...] = _rms(ckv_ref[...].astype(F32), gkv_ref[...]).astype(BF16)
        kpe_ref[...] = _rope(kr_ref[...].astype(F32), cos_ref[...], sin_ref[...]).astype(BF16)

    q = jnp.dot(cqn_ref[...], wq_ref[...], preferred_element_type=F32)
    kv = jnp.dot(ckvn_ref[...], wkv_ref[...], preferred_element_type=F32)
    cos = cos_ref[...]
    sin = sin_ref[...]
    for g in range(heads):
        lo = g * QK_PAD
        mid = lo + LANES
        hi = lo + QK_PAD
        q_out[:, lo:mid] = q[:, lo:mid].astype(BF16)
        q_out[:, mid:hi] = _rope(q[:, mid:hi], cos, sin).astype(BF16)
        k_out[:, lo:mid] = kv[:, lo:mid].astype(BF16)
        k_out[:, mid:hi] = kpe_ref[...]
        v_out[:, lo:mid] = kv[:, mid:hi].astype(BF16)
        v_out[:, mid:hi] = jnp.ones((v_out.shape[0], LANES), BF16)


def _mla_prep(proj, gq, gkv, cos, sin, wq, wkv, bm=512, heads=4):
    s = proj.shape[0]
    bq = heads * QK_PAD
    kern = functools.partial(_mla_prep_kernel, heads=heads)
    return pl.pallas_call(
        kern,
        grid=(s // bm, MLA_HEADS // heads),
        in_specs=[
            pl.BlockSpec((bm, Q_LORA), lambda i, j: (i, COL_CQ // Q_LORA)),
            pl.BlockSpec((bm, KV_LORA), lambda i, j: (i, COL_CKV // KV_LORA)),
            pl.BlockSpec((bm, LANES), lambda i, j: (i, COL_KR // LANES)),
            pl.BlockSpec((1, Q_LORA), lambda i, j: (0, 0)),
            pl.BlockSpec((1, KV_LORA), lambda i, j: (0, 0)),
            pl.BlockSpec((bm, LANES), lambda i, j: (i, 0)),
            pl.BlockSpec((bm, LANES), lambda i, j: (i, 0)),
            pl.BlockSpec((Q_LORA, bq), lambda i, j: (0, j)),
            pl.BlockSpec((KV_LORA, bq), lambda i, j: (0, j)),
        ],
        out_specs=[
            pl.BlockSpec((bm, bq), lambda i, j: (i, j)),
            pl.BlockSpec((bm, bq), lambda i, j: (i, j)),
            pl.BlockSpec((bm, heads * V_PAD), lambda i, j: (i, j)),
        ],
        out_shape=[
            jax.ShapeDtypeStruct((s, MLA_HEADS * QK_PAD), BF16),
            jax.ShapeDtypeStruct((s, MLA_HEADS * QK_PAD), BF16),
            jax.ShapeDtypeStruct((s, MLA_HEADS * V_PAD), BF16),
        ],
        scratch_shapes=[
            pltpu.VMEM((bm, Q_LORA), BF16),
            pltpu.VMEM((bm, KV_LORA), BF16),
            pltpu.VMEM((bm, LANES), BF16),
        ],
        compiler_params=_params("parallel", "arbitrary"),
        name="mla_prep",
    )(proj, proj, proj, gq, gkv, cos, sin, wq, wkv)


_NT = (((1,), (1,)), ((), ()))


def _normalize(acc):
    return acc[:, :V_HEAD] / acc[:, V_HEAD:V_HEAD + 1]


def _mla_attn_kernel(q_ref, k_ref, v_ref, o_ref, *, tk):
    q = q_ref[...]
    m = acc = None
    for c in range(k_ref.shape[0] // tk):
        rows = slice(c * tk, (c + 1) * tk)
        s = lax.dot_general(q, k_ref[rows, :], _NT, preferred_element_type=F32)
        m_c = jnp.max(s, axis=-1, keepdims=True)
        if c == 0:
            m = m_c
            acc = jnp.dot(jnp.exp2(s - m).astype(BF16), v_ref[rows, :], preferred_element_type=F32)
        else:
            m_new = jnp.maximum(m, m_c)
            pv = jnp.dot(jnp.exp2(s - m_new).astype(BF16), v_ref[rows, :], preferred_element_type=F32)
            acc = jnp.exp2(m - m_new) * acc + pv
            m = m_new
    o_ref[...] = _normalize(acc).astype(o_ref.dtype)


def _mla_attn(q, k, v, tq=512, tk=2048):
    s = q.shape[0]
    kern = functools.partial(_mla_attn_kernel, tk=tk)
    return pl.pallas_call(
        kern,
        grid=(MLA_HEADS, s // tq),
        in_specs=[
            pl.BlockSpec((tq, QK_PAD), lambda h, i: (i, h)),
            pl.BlockSpec((s, QK_PAD), lambda h, i: (0, h)),
            pl.BlockSpec((s, V_PAD), lambda h, i: (0, h)),
        ],
        out_specs=pl.BlockSpec((tq, V_HEAD), lambda h, i: (i, h)),
        out_shape=jax.ShapeDtypeStruct((s, MLA_WIDTH), BF16),
        compiler_params=_params("parallel", "arbitrary"),
        name="mla_attn",
    )(q, k, v)


DIL_T = 256
DIL_SUB = 4
DIL_REACH = max(w // 2 for w, _ in DIL_PATTERNS)
DIL_SIDE = DIL_REACH // DIL_T
DIL_NCHUNK = 2 * DIL_SIDE + 1


def _dil_bias(r, slope):
    row = lax.broadcasted_iota(jnp.int32, (DIL_T, DIL_T), 0)
    col = lax.broadcasted_iota(jnp.int32, (DIL_T, DIL_T), 1)
    d = (r - DIL_SIDE) * DIL_T + col - row
    ad = jnp.abs(d)
    cnt = jnp.zeros((DIL_T, DIL_T), F32)
    for window, dil in DIL_PATTERNS:
        member = ((d & (dil - 1)) == 0) & (ad <= window // 2)
        cnt = cnt + member.astype(F32)
    bias = jnp.log2(jnp.maximum(cnt, 1.0)) - (slope * LOG2E) * ad.astype(F32)
    return jnp.where(cnt > 0.0, bias, MASK_BIAS)


def _dil_attn_kernel(slope_ref, q_ref, k_ref, v_ref, o_ref, b_ref, vp_ref, *, nblk):
    h = pl.program_id(0)
    i = pl.program_id(1)

    @pl.when(i == 0)
    def _():
        slope = slope_ref[h]
        for r in range(DIL_NCHUNK):
            b_ref[r] = _dil_bias(r, slope)
        b_ref[DIL_NCHUNK] = jnp.full((DIL_T, DIL_T), MASK_BIAS, F32)
        vp_ref[:, :DIL_HEAD_DIM] = v_ref[...]
        vp_ref[:, DIL_HEAD_DIM:] = jnp.ones((vp_ref.shape[0], V_PAD - DIL_HEAD_DIM), BF16)

    for u in range(DIL_SUB):
        rows = slice(u * DIL_T, (u + 1) * DIL_T)
        q = q_ref[rows, :]
        scores, starts = [], []
        for r in range(DIL_NCHUNK):
            kc = i * DIL_SUB + u + (r - DIL_SIDE)
            inside = (kc >= 0) & (kc < nblk)
            ks = pl.multiple_of(jnp.clip(kc, 0, nblk - 1) * DIL_T, DIL_T)
            bias = b_ref[jnp.where(inside, r, DIL_NCHUNK)]
            scores.append(lax.dot_general(q, k_ref[pl.ds(ks, DIL_T), :], _NT, preferred_element_type=F32) + bias)
            starts.append(ks)
        m = functools.reduce(jnp.maximum, [jnp.max(s, axis=-1, keepdims=True) for s in scores])
        acc = None
        for s, ks in zip(scores, starts):
            pv = jnp.dot(jnp.exp2(s - m).astype(BF16), vp_ref[pl.ds(ks, DIL_T), :], preferred_element_type=F32)
            acc = pv if acc is None else acc + pv
        o_ref[rows, :] = _normalize(acc).astype(o_ref.dtype)


def _dil_attn(proj, slopes):
    s = proj.shape[0]
    nblk = s // DIL_T
    kern = functools.partial(_dil_attn_kernel, nblk=nblk)
    hd = DIL_HEAD_DIM
    return pl.pallas_call(
        kern,
        grid=(DIL_HEADS, nblk // DIL_SUB),
        in_specs=[
            pl.BlockSpec(memory_space=pltpu.SMEM),
            pl.BlockSpec((DIL_T * DIL_SUB, hd), lambda h, i: (i, COL_QB // hd + h)),
            pl.BlockSpec((s, hd), lambda h, i: (0, COL_KB // hd + h)),
            pl.BlockSpec((s, hd), lambda h, i: (0, COL_VB // hd + h)),
        ],
        out_specs=pl.BlockSpec((DIL_T * DIL_SUB, hd), lambda h, i: (i, h)),
        out_shape=jax.ShapeDtypeStruct((s, DIL_WIDTH), BF16),
        scratch_shapes=[
            pltpu.VMEM((DIL_NCHUNK + 1, DIL_T, DIL_T), F32),
            pltpu.VMEM((s, V_PAD), BF16),
        ],
        compiler_params=_params("arbitrary", "arbitrary"),
        name="dil_attn",
    )(slopes, proj, proj, proj)


def _gated_norm(a_ref, gate_ref, g_ref, y_ref):
    for r in range(0, a_ref.shape[0], ROW_CHUNK):
        rows = slice(r, r + ROW_CHUNK)
        gate = gate_ref[rows, :].astype(F32)
        y = _rms(a_ref[rows, :].astype(F32), g_ref[...]) * (gate * jax.nn.sigmoid(gate))
        y_ref[rows, :] = y.astype(y_ref.dtype)


def _out_proj_kernel(aa_ref, ab_ref, ga_ref, gb_ref, gma_ref, gmb_ref, x_ref, w_ref, o_ref, y_ref):
    @pl.when(pl.program_id(1) == 0)
    def _():
        _gated_norm(aa_ref, ga_ref, gma_ref, y_ref.at[:, :MLA_WIDTH])
        _gated_norm(ab_ref, gb_ref, gmb_ref, y_ref.at[:, MLA_WIDTH:])

    o_ref[...] = x_ref[...] + jnp.dot(y_ref[...], w_ref[...], preferred_element_type=F32)


def _out_proj(attn_a, attn_b, proj, g_a, g_b, x, w, bm=512, bn=512):
    s, d = x.shape
    mix = MLA_WIDTH + DIL_WIDTH
    return pl.pallas_call(
        _out_proj_kernel,
        grid=(s // bm, d // bn),
        in_specs=[
            pl.BlockSpec((bm, MLA_WIDTH), lambda i, j: (i, 0)),
            pl.BlockSpec((bm, DIL_WIDTH), lambda i, j: (i, 0)),
            pl.BlockSpec((bm, MLA_WIDTH), lambda i, j: (i, COL_GATE_A // MLA_WIDTH)),
            pl.BlockSpec((bm, DIL_WIDTH), lambda i, j: (i, COL_GATE_B // DIL_WIDTH)),
            pl.BlockSpec((1, MLA_WIDTH), lambda i, j: (0, 0)),
            pl.BlockSpec((1, DIL_WIDTH), lambda i, j: (0, 0)),
            pl.BlockSpec((bm, bn), lambda i, j: (i, j)),
            pl.BlockSpec((mix, bn), lambda i, j: (0, j)),
        ],
        out_specs=pl.BlockSpec((bm, bn), lambda i, j: (i, j)),
        out_shape=jax.ShapeDtypeStruct((s, d), F32),
        scratch_shapes=[pltpu.VMEM((bm, mix), BF16)],
        compiler_params=_params("parallel", "arbitrary"),
        name="out_proj",
    )(attn_a, attn_b, proj, proj, g_a, g_b, x, w)


def _ple_gate_kernel(xrow_ref, g_ref, wg_ref, p_ref, wp_ref, xt_ref, o_ref, h_ref):
    @pl.when(pl.program_id(1) == 0)
    def _():
        _rms_rows(xrow_ref, g_ref, h_ref)

    gate = jax.nn.sigmoid(jnp.dot(h_ref[...], wg_ref[...], preferred_element_type=F32))
    ple = jnp.dot(p_ref[...].astype(BF16), wp_ref[...], preferred_element_type=F32)
    o_ref[...] = xt_ref[...] + ple * gate


def _ple_gate(x, g, wg, p, wp, bm=512, bn=512):
    s, d = x.shape
    c = p.shape[1]
    return pl.pallas_call(
        _ple_gate_kernel,
        grid=(s // bm, d // bn),
        in_specs=[
            pl.BlockSpec((bm, d), lambda i, j: (i, 0)),
            pl.BlockSpec((1, d), lambda i, j: (0, 0)),
            pl.BlockSpec((d, bn), lambda i, j: (0, j)),
            pl.BlockSpec((bm, c), lambda i, j: (i, 0)),
            pl.BlockSpec((c, bn), lambda i, j: (0, j)),
            pl.BlockSpec((bm, bn), lambda i, j: (i, j)),
        ],
        out_specs=pl.BlockSpec((bm, bn), lambda i, j: (i, j)),
        out_shape=jax.ShapeDtypeStruct((s, d), F32),
        scratch_shapes=[pltpu.VMEM((bm, d), BF16)],
        compiler_params=_params("parallel", "arbitrary"),
        name="ple_gate",
    )(x, g, wg, p, wp, x)


def _final_norm_kernel(x_ref, g_ref, o_ref):
    o_ref[...] = _rms(x_ref[...], g_ref[...])


def _final_norm(x, g, bm=256):
    s, d = x.shape
    return pl.pallas_call(
        _final_norm_kernel,
        grid=(s // bm,),
        in_specs=[pl.BlockSpec((bm, d), lambda i: (i, 0)), pl.BlockSpec((1, d), lambda i: (0, 0))],
        out_specs=pl.BlockSpec((bm, d), lambda i: (i, 0)),
        out_shape=jax.ShapeDtypeStruct((s, d), F32),
        compiler_params=_params("parallel"),
        name="final_norm",
    )(x, g)


def _rope_dup(w):
    t1, t2 = w[..., :ROPE_HALF], w[..., ROPE_HALF:]
    return jnp.concatenate([t1, t2, t2, t1], axis=-1)


def _prep_w_in(w):
    d = w.shape[0]
    offs = np.cumsum((0, Q_LORA, KV_LORA, QK_ROPE, MLA_WIDTH, DIL_WIDTH, DIL_WIDTH, DIL_WIDTH, DIL_WIDTH))
    c_q, c_kv, k_r, gate_a, q_b, k_b, v_b, gate_b = [w[:, offs[n]:offs[n + 1]] for n in range(8)]
    q_b = q_b * (DIL_HEAD_DIM ** -0.5 * LOG2E)
    pad = jnp.zeros((d, PROJ_WIDTH - COL_KR - LANES), w.dtype)
    return jnp.concatenate([gate_a, q_b, k_b, v_b, gate_b, c_q, c_kv, _rope_dup(k_r), pad], axis=1).astype(BF16)


def _prep_w_uq(w):
    r = w.shape[0]
    w = w.reshape(r, MLA_HEADS, QK_NOPE + QK_ROPE) * ((QK_NOPE + QK_ROPE) ** -0.5 * LOG2E)
    w = jnp.concatenate([w[..., :QK_NOPE], _rope_dup(w[..., QK_NOPE:])], axis=-1)
    return w.reshape(r, MLA_HEADS * QK_PAD).astype(BF16)


def _rope_tables(positions):
    inv = ROPE_THETA ** (-jnp.arange(ROPE_HALF, dtype=F32) / ROPE_HALF)
    ang = positions.astype(F32)[:, None] * inv
    cos, sin = jnp.cos(ang), jnp.sin(ang)
    zero = jnp.zeros_like(cos)
    return (jnp.concatenate([cos, cos, zero, zero], axis=1),
            jnp.concatenate([-sin, sin, zero, zero], axis=1))


def kernel(x, p, positions, g_mix, w_in, g_q_latent, w_uq, g_kv_latent, w_ukv, g_out_mla, g_out_dil,
           w_out, w_ple, g_ple, w_ple_gate, g_final):
    b, s, d = x.shape
    assert b == 1 and s % (DIL_T * DIL_SUB) == 0
    slopes = jnp.asarray(2.0 ** (-ALIBI_MAX_BIAS * np.arange(1, DIL_HEADS + 1) / DIL_HEADS), F32)
    x2d = x.reshape(s, d)
    cos, sin = _rope_tables(positions[0])
    for i in range(g_mix.shape[0]):
        proj = _in_proj(x2d, g_mix[i][None], _prep_w_in(w_in[i]))
        q_a, k_a, v_a = _mla_prep(proj, g_q_latent[i][None], g_kv_latent[i][None], cos, sin,
                                  _prep_w_uq(w_uq[i]), w_ukv[i].astype(BF16))
        attn_a = _mla_attn(q_a, k_a, v_a)
        attn_b = _dil_attn(proj, slopes)
        x2d = _out_proj(attn_a, attn_b, proj, g_out_mla[i][None], g_out_dil[i][None], x2d,
                        w_out[i].astype(BF16))
        x2d = _ple_gate(x2d, g_ple[i][None], w_ple_gate[i].astype(BF16), p[i, 0], w_ple[i].astype(BF16))
    return _final_norm(x2d, g_final[None]).reshape(b, s, d)
```

```python
import functools

import numpy as np
import jax
import jax.numpy as jnp
from jax import lax
from jax.experimental import pallas as pl
from jax.experimental.pallas import tpu as pltpu

F32 = jnp.float32
BF16 = jnp.bfloat16

EPS = 1e-6
ROPE_THETA = 10000.0
ALIBI_MAX_BIAS = 8.0

MLA_HEADS = 16
Q_LORA = 1024
KV_LORA = 512
QK_NOPE = 128
QK_ROPE = 64
V_HEAD = 128
MLA_WIDTH = MLA_HEADS * V_HEAD
DIL_HEADS = 16
DIL_HEAD_DIM = 128
DIL_WIDTH = DIL_HEADS * DIL_HEAD_DIM
DIL_PATTERNS = ((128, 1), (512, 4), (2048, 16))

LANES = 128
QK_PAD = 256
V_PAD = 2 * LANES
ROPE_HALF = QK_ROPE // 2
LOG2E = float(np.log2(np.e))

COL_GATE_A = 0
COL_QB = COL_GATE_A + MLA_WIDTH
COL_KB = COL_QB + DIL_WIDTH
COL_VB = COL_KB + DIL_WIDTH
COL_GATE_B = COL_VB + DIL_WIDTH
COL_CQ = COL_GATE_B + DIL_WIDTH
COL_CKV = COL_CQ + Q_LORA
COL_KR = COL_CKV + KV_LORA
PROJ_WIDTH = 12288

MASK_BIAS = -1e30

ROW_CHUNK = 64

VMEM_LIMIT = 60 * 1024 * 1024


def _params(*sem):
    return pltpu.CompilerParams(dimension_semantics=sem, vmem_limit_bytes=VMEM_LIMIT)


def _rms(x, g):
    return x * lax.rsqrt(jnp.mean(x * x, axis=-1, keepdims=True) + EPS) * g


def _rms_rows(x_ref, g_ref, h_ref):
    for r in range(0, x_ref.shape[0], ROW_CHUNK):
        rows = slice(r, r + ROW_CHUNK)
        h_ref[rows, :] = _rms(x_ref[rows, :].astype(F32), g_ref[...]).astype(h_ref.dtype)


def _in_proj_kernel(x_ref, g_ref, wm_ref, wt_ref, o_ref, h_ref, *, n_main):
    j = pl.program_id(1)

    @pl.when(j == 0)
    def _():
        _rms_rows(x_ref, g_ref, h_ref)

    @pl.when(j < n_main)
    def _():
        o_ref[...] = jnp.dot(h_ref[...], wm_ref[...], preferred_element_type=F32).astype(o_ref.dtype)

    @pl.when(j >= n_main)
    def _():
        o_ref[...] = jnp.dot(h_ref[...], wt_ref[...], preferred_element_type=F32).astype(o_ref.dtype)


def _in_proj(x, g, w_main, w_tail, bm=512, bn=512):
    s, d = x.shape
    n_main = w_main.shape[1] // bn
    n_tail = w_tail.shape[1] // bn
    kern = functools.partial(_in_proj_kernel, n_main=n_main)
    return pl.pallas_call(
        kern,
        grid=(s // bm, n_main + n_tail),
        in_specs=[
            pl.BlockSpec((bm, d), lambda i, j: (i, 0)),
            pl.BlockSpec((1, d), lambda i, j: (0, 0)),
            pl.BlockSpec((d, bn), lambda i, j: (0, jnp.minimum(j, n_main - 1))),
            pl.BlockSpec((d, bn), lambda i, j: (0, jnp.maximum(j - n_main, 0))),
        ],
        out_specs=pl.BlockSpec((bm, bn), lambda i, j: (i, j)),
        out_shape=jax.ShapeDtypeStruct((s, (n_main + n_tail) * bn), BF16),
        scratch_shapes=[pltpu.VMEM((bm, d), BF16)],
        compiler_params=_params("parallel", "arbitrary"),
        name="in_proj",
    )(x, g, w_main, w_tail)


def _rope(t, cos, sin):
    return t * cos + pltpu.roll(t, 2 * ROPE_HALF, axis=1) * sin


def _mla_prep_kernel(cq_ref, ckv_ref, kr_ref, gq_ref, gkv_ref, cos_ref, sin_ref, wq_ref, wkv_ref,
                     q_out, k_out, v_out, cqn_ref, ckvn_ref, kpe_ref, *, heads):
    @pl.when(pl.program_id(1) == 0)
    def _():
        cqn_ref[...] = _rms(cq_ref[...].astype(F32), gq_ref[...]).astype(BF16)
        ckvn_ref[...] = _rms(ckv_ref[...].astype(F32), gkv_ref[...]).astype(BF16)
        kpe_ref[...] = _rope(kr_ref[...].astype(F32), cos_ref[...], sin_ref[...]).astype(BF16)

    q = jnp.dot(cqn_ref[...], wq_ref[...], preferred_element_type=F32)
    kv = jnp.dot(ckvn_ref[...], wkv_ref[...], preferred_element_type=F32)
    cos = cos_ref[...]
    sin = sin_ref[...]
    for g in range(heads):
        lo = g * QK_PAD
        mid = lo + LANES
        hi = lo + QK_PAD
        q_out[:, lo:mid] = q[:, lo:mid].astype(BF16)
        q_out[:, mid:hi] = _rope(q[:, mid:hi], cos, sin).astype(BF16)
        k_out[:, lo:mid] = kv[:, lo:mid].astype(BF16)
        k_out[:, mid:hi] = kpe_ref[...]
        v_out[:, lo:mid] = kv[:, mid:hi].astype(BF16)
        v_out[:, mid:hi] = jnp.ones((v_out.shape[0], LANES), BF16)


def _mla_prep(proj, gq, gkv, cos, sin, wq, wkv, bm=512, heads=4):
    s = proj.shape[0]
    bq = heads * QK_PAD
    kern = functools.partial(_mla_prep_kernel, heads=heads)
    return pl.pallas_call(
        kern,
        grid=(s // bm, MLA_HEADS // heads),
        in_specs=[
            pl.BlockSpec((bm, Q_LORA), lambda i, j: (i, COL_CQ // Q_LORA)),
            pl.BlockSpec((bm, KV_LORA), lambda i, j: (i, COL_CKV // KV_LORA)),
            pl.BlockSpec((bm, LANES), lambda i, j: (i, COL_KR // LANES)),
            pl.BlockSpec((1, Q_LORA), lambda i, j: (0, 0)),
            pl.BlockSpec((1, KV_LORA), lambda i, j: (0, 0)),
            pl.BlockSpec((bm, LANES), lambda i, j: (i, 0)),
            pl.BlockSpec((bm, LANES), lambda i, j: (i, 0)),
            pl.BlockSpec((Q_LORA, bq), lambda i, j: (0, j)),
            pl.BlockSpec((KV_LORA, bq), lambda i, j: (0, j)),
        ],
        out_specs=[
            pl.BlockSpec((bm, bq), lambda i, j: (i, j)),
            pl.BlockSpec((bm, bq), lambda i, j: (i, j)),
            pl.BlockSpec((bm, heads * V_PAD), lambda i, j: (i, j)),
        ],
        out_shape=[
            jax.ShapeDtypeStruct((s, MLA_HEADS * QK_PAD), BF16),
            jax.ShapeDtypeStruct((s, MLA_HEADS * QK_PAD), BF16),
            jax.ShapeDtypeStruct((s, MLA_HEADS * V_PAD), BF16),
        ],
        scratch_shapes=[
            pltpu.VMEM((bm, Q_LORA), BF16),
            pltpu.VMEM((bm, KV_LORA), BF16),
            pltpu.VMEM((bm, LANES), BF16),
        ],
        compiler_params=_params("parallel", "arbitrary"),
        name="mla_prep",
    )(proj, proj, proj, gq, gkv, cos, sin, wq, wkv)


_NT = (((1,), (1,)), ((), ()))


def _normalize(acc):
    return acc[:, :V_HEAD] / acc[:, V_HEAD:V_HEAD + 1]


def _mla_attn_kernel(q_ref, k_ref, v_ref, o_ref, *, tk):
    q = q_ref[...]
    m = acc = None
    for c in range(k_ref.shape[0] // tk):
        rows = slice(c * tk, (c + 1) * tk)
        s = lax.dot_general(q, k_ref[rows, :], _NT, preferred_element_type=F32)
        m_c = jnp.max(s, axis=-1, keepdims=True)
        if c == 0:
            m = m_c
            acc = jnp.dot(jnp.exp2(s - m).astype(BF16), v_ref[rows, :], preferred_element_type=F32)
        else:
            m_new = jnp.maximum(m, m_c)
            pv = jnp.dot(jnp.exp2(s - m_new).astype(BF16), v_ref[rows, :], preferred_element_type=F32)
            acc = jnp.exp2(m - m_new) * acc + pv
            m = m_new
    o_ref[...] = _normalize(acc).astype(o_ref.dtype)


def _mla_attn(q, k, v, tq=512, tk=2048):
    s = q.shape[0]
    kern = functools.partial(_mla_attn_kernel, tk=tk)
    return pl.pallas_call(
        kern,
        grid=(MLA_HEADS, s // tq),
        in_specs=[
            pl.BlockSpec((tq, QK_PAD), lambda h, i: (i, h)),
            pl.BlockSpec((s, QK_PAD), lambda h, i: (0, h)),
            pl.BlockSpec((s, V_PAD), lambda h, i: (0, h)),
        ],
        out_specs=pl.BlockSpec((tq, V_HEAD), lambda h, i: (i, h)),
        out_shape=jax.ShapeDtypeStruct((s, MLA_WIDTH), BF16),
        compiler_params=_params("parallel", "arbitrary"),
        name="mla_attn",
    )(q, k, v)


DIL_T = 256
DIL_SUB = 4
DIL_REACH = max(w // 2 for w, _ in DIL_PATTERNS)
DIL_SIDE = DIL_REACH // DIL_T
DIL_NCHUNK = 2 * DIL_SIDE + 1


def _dil_bias(r, slope):
    row = lax.broadcasted_iota(jnp.int32, (DIL_T, DIL_T), 0)
    col = lax.broadcasted_iota(jnp.int32, (DIL_T, DIL_T), 1)
    d = (r - DIL_SIDE) * DIL_T + col - row
    ad = jnp.abs(d)
    cnt = jnp.zeros((DIL_T, DIL_T), F32)
    for window, dil in DIL_PATTERNS:
        member = ((d & (dil - 1)) == 0) & (ad <= window // 2)
        cnt = cnt + member.astype(F32)
    bias = jnp.log2(jnp.maximum(cnt, 1.0)) - (slope * LOG2E) * ad.astype(F32)
    return jnp.where(cnt > 0.0, bias, MASK_BIAS)


def _dil_attn_kernel(slope_ref, q_ref, k_ref, v_ref, o_ref, b_ref, vp_ref, *, nblk):
    h = pl.program_id(0)
    i = pl.program_id(1)

    @pl.when(i == 0)
    def _():
        slope = slope_ref[h]
        for r in range(DIL_NCHUNK):
            b_ref[r] = _dil_bias(r, slope)
        b_ref[DIL_NCHUNK] = jnp.full((DIL_T, DIL_T), MASK_BIAS, F32)
        vp_ref[:, :DIL_HEAD_DIM] = v_ref[...]
        vp_ref[:, DIL_HEAD_DIM:] = jnp.ones((vp_ref.shape[0], V_PAD - DIL_HEAD_DIM), BF16)

    for u in range(DIL_SUB):
        rows = slice(u * DIL_T, (u + 1) * DIL_T)
        q = q_ref[rows, :]
        scores, starts = [], []
        for r in range(DIL_NCHUNK):
            kc = i * DIL_SUB + u + (r - DIL_SIDE)
            inside = (kc >= 0) & (kc < nblk)
            ks = pl.multiple_of(jnp.clip(kc, 0, nblk - 1) * DIL_T, DIL_T)
            bias = b_ref[jnp.where(inside, r, DIL_NCHUNK)]
            scores.append(lax.dot_general(q, k_ref[pl.ds(ks, DIL_T), :], _NT, preferred_element_type=F32) + bias)
            starts.append(ks)
        m = functools.reduce(jnp.maximum, [jnp.max(s, axis=-1, keepdims=True) for s in scores])
        acc = None
        for s, ks in zip(scores, starts):
            pv = jnp.dot(jnp.exp2(s - m).astype(BF16), vp_ref[pl.ds(ks, DIL_T), :], preferred_element_type=F32)
            acc = pv if acc is None else acc + pv
        o_ref[rows, :] = _normalize(acc).astype(o_ref.dtype)


def _dil_attn(proj, slopes):
    s = proj.shape[0]
    nblk = s // DIL_T
    kern = functools.partial(_dil_attn_kernel, nblk=nblk)
    hd = DIL_HEAD_DIM
    return pl.pallas_call(
        kern,
        grid=(DIL_HEADS, nblk // DIL_SUB),
        in_specs=[
            pl.BlockSpec(memory_space=pltpu.SMEM),
            pl.BlockSpec((DIL_T * DIL_SUB, hd), lambda h, i: (i, COL_QB // hd + h)),
            pl.BlockSpec((s, hd), lambda h, i: (0, COL_KB // hd + h)),
            pl.BlockSpec((s, hd), lambda h, i: (0, COL_VB // hd + h)),
        ],
        out_specs=pl.BlockSpec((DIL_T * DIL_SUB, hd), lambda h, i: (i, h)),
        out_shape=jax.ShapeDtypeStruct((s, DIL_WIDTH), BF16),
        scratch_shapes=[
            pltpu.VMEM((DIL_NCHUNK + 1, DIL_T, DIL_T), F32),
            pltpu.VMEM((s, V_PAD), BF16),
        ],
        compiler_params=_params("arbitrary", "arbitrary"),
        name="dil_attn",
    )(slopes, proj, proj, proj)


def _gated_norm(a_ref, gate_ref, g_ref, y_ref):
    for r in range(0, a_ref.shape[0], ROW_CHUNK):
        rows = slice(r, r + ROW_CHUNK)
        gate = gate_ref[rows, :].astype(F32)
        y = _rms(a_ref[rows, :].astype(F32), g_ref[...]) * (gate * jax.nn.sigmoid(gate))
        y_ref[rows, :] = y.astype(y_ref.dtype)


def _tail_kernel(aa_ref, ab_ref, ga_ref, gb_ref, gma_ref, gmb_ref, x_ref, w_ref, gple_ref, p_ref, wp_ref,
                 gfin_ref, o_ref, yh_ref, *, nj, bn, last_layer):
    jj = pl.program_id(1)

    @pl.when(jj == 0)
    def _():
        _gated_norm(aa_ref, ga_ref, gma_ref, yh_ref.at[:, :MLA_WIDTH])
        _gated_norm(ab_ref, gb_ref, gmb_ref, yh_ref.at[:, MLA_WIDTH:])

    @pl.when(jj < nj)
    def _():
        cols = pl.ds(pl.multiple_of(jj * bn, bn), bn)
        o_ref[:, cols] = x_ref[...] + jnp.dot(yh_ref[...], w_ref[...], preferred_element_type=F32)

    @pl.when(jj == nj)
    def _():
        _rms_rows(o_ref, gple_ref, yh_ref)

    @pl.when(jj >= nj)
    def _():
        cols = pl.ds(pl.multiple_of((jj - nj) * bn, bn), bn)
        gate = jax.nn.sigmoid(jnp.dot(yh_ref[...], w_ref[...], preferred_element_type=F32))
        ple = jnp.dot(p_ref[...].astype(BF16), wp_ref[...], preferred_element_type=F32)
        o_ref[:, cols] = o_ref[:, cols] + ple * gate

    if last_layer:
        @pl.when(jj == 2 * nj - 1)
        def _():
            _rms_rows(o_ref, gfin_ref, o_ref)


def _tail(attn_a, attn_b, proj, g_a, g_b, x, w_cat, g_ple, p, w_ple, g_fin, last_layer, bm=512, bn=512):
    s, d = x.shape
    assert d == MLA_WIDTH + DIL_WIDTH
    c = p.shape[1]
    nj = d // bn
    kern = functools.partial(_tail_kernel, nj=nj, bn=bn, last_layer=last_layer)
    first = lambda i, jj: (i, jnp.minimum(jj, nj - 1))
    second = lambda i, jj: (0, jnp.maximum(jj - nj, 0))
    row = lambda i, jj: (i, 0)
    vec = lambda i, jj: (0, 0)
    return pl.pallas_call(
        kern,
        grid=(s // bm, 2 * nj),
        in_specs=[
            pl.BlockSpec((bm, MLA_WIDTH), row),
            pl.BlockSpec((bm, DIL_WIDTH), row),
            pl.BlockSpec((bm, MLA_WIDTH), lambda i, jj: (i, COL_GATE_A // MLA_WIDTH)),
            pl.BlockSpec((bm, DIL_WIDTH), lambda i, jj: (i, COL_GATE_B // DIL_WIDTH)),
            pl.BlockSpec((1, MLA_WIDTH), vec),
            pl.BlockSpec((1, DIL_WIDTH), vec),
            pl.BlockSpec((bm, bn), first),
            pl.BlockSpec((d, bn), lambda i, jj: (jj // nj, jj % nj)),
            pl.BlockSpec((1, d), vec),
            pl.BlockSpec((bm, c), row),
            pl.BlockSpec((c, bn), second),
            pl.BlockSpec((1, d), vec),
        ],
        out_specs=pl.BlockSpec((bm, d), row),
        out_shape=jax.ShapeDtypeStruct((s, d), F32),
        scratch_shapes=[pltpu.VMEM((bm, d), BF16)],
        compiler_params=_params("parallel", "arbitrary"),
        name="tail",
    )(attn_a, attn_b, proj, proj, g_a, g_b, x, w_cat, g_ple, p, w_ple, g_fin)


def _rope_dup(w):
    t1, t2 = w[..., :ROPE_HALF], w[..., ROPE_HALF:]
    return jnp.concatenate([t1, t2, t2, t1], axis=-1)


def _prep_w_in(w):
    d = w.shape[0]
    lat = Q_LORA + KV_LORA
    head = lat + QK_ROPE
    col_scale = np.ones((w.shape[1] - head,), np.float32)
    col_scale[COL_QB:COL_KB] = DIL_HEAD_DIM ** -0.5 * LOG2E
    w_main = (w[:, head:] * col_scale).astype(BF16)
    pad = jnp.zeros((d, PROJ_WIDTH - COL_KR - LANES), w.dtype)
    w_tail = jnp.concatenate([w[:, :lat], _rope_dup(w[:, lat:head]), pad], axis=1).astype(BF16)
    return w_main, w_tail


def _prep_w_uq(w):
    r = w.shape[0]
    w = w.reshape(r, MLA_HEADS, QK_NOPE + QK_ROPE) * ((QK_NOPE + QK_ROPE) ** -0.5 * LOG2E)
    w = jnp.concatenate([w[..., :QK_NOPE], _rope_dup(w[..., QK_NOPE:])], axis=-1)
    return w.reshape(r, MLA_HEADS * QK_PAD).astype(BF16)


def _rope_tables(positions):
    inv = ROPE_THETA ** (-jnp.arange(ROPE_HALF, dtype=F32) / ROPE_HALF)
    ang = positions.astype(F32)[:, None] * inv
    cos, sin = jnp.cos(ang), jnp.sin(ang)
    zero = jnp.zeros_like(cos)
    return (jnp.concatenate([cos, cos, zero, zero], axis=1),
            jnp.concatenate([-sin, sin, zero, zero], axis=1))


def kernel(x, p, positions, g_mix, w_in, g_q_latent, w_uq, g_kv_latent, w_ukv, g_out_mla, g_out_dil,
           w_out, w_ple, g_ple, w_ple_gate, g_final):
    b, s, d = x.shape
    assert b == 1 and s % (DIL_T * DIL_SUB) == 0
    slopes = jnp.asarray(2.0 ** (-ALIBI_MAX_BIAS * np.arange(1, DIL_HEADS + 1) / DIL_HEADS), F32)
    x2d = x.reshape(s, d)
    cos, sin = _rope_tables(positions[0])
    depth = g_mix.shape[0]
    for i in range(depth):
        proj = _in_proj(x2d, g_mix[i][None], *_prep_w_in(w_in[i]))
        q_a, k_a, v_a = _mla_prep(proj, g_q_latent[i][None], g_kv_latent[i][None], cos, sin,
                                  _prep_w_uq(w_uq[i]), w_ukv[i].astype(BF16))
        attn_a = _mla_attn(q_a, k_a, v_a)
        attn_b = _dil_attn(proj, slopes)
        w_cat = jnp.concatenate([w_out[i].astype(BF16), w_ple_gate[i].astype(BF16)], axis=0)
        x2d = _tail(attn_a, attn_b, proj, g_out_mla[i][None], g_out_dil[i][None], x2d, w_cat,
                    g_ple[i][None], p[i, 0], w_ple[i].astype(BF16), g_final[None], i == depth - 1)
    return x2d.reshape(b, s, d)
```

```python
import functools

import numpy as np
import jax
import jax.numpy as jnp
from jax import lax
from jax.experimental import pallas as pl
from jax.experimental.pallas import tpu as pltpu

F32 = jnp.float32
BF16 = jnp.bfloat16

EPS = 1e-6
ROPE_THETA = 10000.0
ALIBI_MAX_BIAS = 8.0

MLA_HEADS = 16
Q_LORA = 1024
KV_LORA = 512
QK_NOPE = 128
QK_ROPE = 64
V_HEAD = 128
MLA_WIDTH = MLA_HEADS * V_HEAD
DIL_HEADS = 16
DIL_HEAD_DIM = 128
DIL_WIDTH = DIL_HEADS * DIL_HEAD_DIM
DIL_PATTERNS = ((128, 1), (512, 4), (2048, 16))

LANES = 128
QK_PAD = 256
V_PAD = 2 * LANES
ROPE_HALF = QK_ROPE // 2
LOG2E = float(np.log2(np.e))

COL_GATE_A = 0
COL_QB = COL_GATE_A + MLA_WIDTH
COL_KB = COL_QB + DIL_WIDTH
COL_VB = COL_KB + DIL_WIDTH
COL_GATE_B = COL_VB + DIL_WIDTH
COL_CQ = COL_GATE_B + DIL_WIDTH
COL_CKV = COL_CQ + Q_LORA
COL_KR = COL_CKV + KV_LORA
PROJ_WIDTH = 12288

MASK_BIAS = -1e30

ROW_CHUNK = 64

VMEM_LIMIT = 60 * 1024 * 1024


def _params(*sem):
    return pltpu.CompilerParams(dimension_semantics=sem, vmem_limit_bytes=VMEM_LIMIT)


def _rms(x, g):
    return x * lax.rsqrt(jnp.mean(x * x, axis=-1, keepdims=True) + EPS) * g


def _rms_rows(x_ref, g_ref, h_ref):
    for r in range(0, x_ref.shape[0], ROW_CHUNK):
        rows = slice(r, r + ROW_CHUNK)
        h_ref[rows, :] = _rms(x_ref[rows, :].astype(F32), g_ref[...]).astype(h_ref.dtype)


_NT = (((1,), (1,)), ((), ()))


def _in_proj_kernel(x_ref, g_ref, wm_ref, wt_ref, o_ref, h_ref, *, n_main):
    j = pl.program_id(1)

    @pl.when(j == 0)
    def _():
        _rms_rows(x_ref, g_ref, h_ref)

    @pl.when(j < n_main)
    def _():
        o_ref[...] = lax.dot_general(h_ref[...], wm_ref[...], _NT, preferred_element_type=F32).astype(o_ref.dtype)

    @pl.when(j >= n_main)
    def _():
        o_ref[...] = lax.dot_general(h_ref[...], wt_ref[...], _NT, preferred_element_type=F32).astype(o_ref.dtype)


def _in_proj(x, g, w_main, w_tail, bm=512, bn=512):
    s, d = x.shape
    n_main = w_main.shape[0] // bn
    n_tail = w_tail.shape[0] // bn
    kern = functools.partial(_in_proj_kernel, n_main=n_main)
    return pl.pallas_call(
        kern,
        grid=(s // bm, n_main + n_tail),
        in_specs=[
            pl.BlockSpec((bm, d), lambda i, j: (i, 0)),
            pl.BlockSpec((1, d), lambda i, j: (0, 0)),
            pl.BlockSpec((bn, d), lambda i, j: (jnp.minimum(j, n_main - 1), 0)),
            pl.BlockSpec((bn, d), lambda i, j: (jnp.maximum(j - n_main, 0), 0)),
        ],
        out_specs=pl.BlockSpec((bm, bn), lambda i, j: (i, j)),
        out_shape=jax.ShapeDtypeStruct((s, (n_main + n_tail) * bn), BF16),
        scratch_shapes=[pltpu.VMEM((bm, d), BF16)],
        compiler_params=_params("parallel", "arbitrary"),
        name="in_proj",
    )(x, g, w_main, w_tail)


def _rope(t, cos, sin):
    return t * cos + pltpu.roll(t, 2 * ROPE_HALF, axis=1) * sin


def _mla_prep_kernel(cq_ref, ckv_ref, kr_ref, gq_ref, gkv_ref, cos_ref, sin_ref, wq_ref, wkv_ref,
                     q_out, k_out, v_out, cqn_ref, ckvn_ref, kpe_ref, *, heads):
    @pl.when(pl.program_id(1) == 0)
    def _():
        cqn_ref[...] = _rms(cq_ref[...].astype(F32), gq_ref[...]).astype(BF16)
        ckvn_ref[...] = _rms(ckv_ref[...].astype(F32), gkv_ref[...]).astype(BF16)
        kpe_ref[...] = _rope(kr_ref[...].astype(F32), cos_ref[...], sin_ref[...]).astype(BF16)

    q = jnp.dot(cqn_ref[...], wq_ref[...], preferred_element_type=F32)
    kv = jnp.dot(ckvn_ref[...], wkv_ref[...], preferred_element_type=F32)
    cos = cos_ref[...]
    sin = sin_ref[...]
    for g in range(heads):
        lo = g * QK_PAD
        mid = lo + LANES
        hi = lo + QK_PAD
        q_out[:, lo:mid] = q[:, lo:mid].astype(BF16)
        q_out[:, mid:hi] = _rope(q[:, mid:hi], cos, sin).astype(BF16)
        k_out[:, lo:mid] = kv[:, lo:mid].astype(BF16)
        k_out[:, mid:hi] = kpe_ref[...]
        v_out[:, lo:mid] = kv[:, mid:hi].astype(BF16)
        v_out[:, mid:hi] = jnp.ones((v_out.shape[0], LANES), BF16)


def _mla_prep(proj, gq, gkv, cos, sin, wq, wkv, bm=512, heads=4):
    s = proj.shape[0]
    bq = heads * QK_PAD
    kern = functools.partial(_mla_prep_kernel, heads=heads)
    return pl.pallas_call(
        kern,
        grid=(s // bm, MLA_HEADS // heads),
        in_specs=[
            pl.BlockSpec((bm, Q_LORA), lambda i, j: (i, COL_CQ // Q_LORA)),
            pl.BlockSpec((bm, KV_LORA), lambda i, j: (i, COL_CKV // KV_LORA)),
            pl.BlockSpec((bm, LANES), lambda i, j: (i, COL_KR // LANES)),
            pl.BlockSpec((1, Q_LORA), lambda i, j: (0, 0)),
            pl.BlockSpec((1, KV_LORA), lambda i, j: (0, 0)),
            pl.BlockSpec((bm, LANES), lambda i, j: (i, 0)),
            pl.BlockSpec((bm, LANES), lambda i, j: (i, 0)),
            pl.BlockSpec((Q_LORA, bq), lambda i, j: (0, j)),
            pl.BlockSpec((KV_LORA, bq), lambda i, j: (0, j)),
        ],
        out_specs=[
            pl.BlockSpec((bm, bq), lambda i, j: (i, j)),
            pl.BlockSpec((bm, bq), lambda i, j: (i, j)),
            pl.BlockSpec((bm, heads * V_PAD), lambda i, j: (i, j)),
        ],
        out_shape=[
            jax.ShapeDtypeStruct((s, MLA_HEADS * QK_PAD), BF16),
            jax.ShapeDtypeStruct((s, MLA_HEADS * QK_PAD), BF16),
            jax.ShapeDtypeStruct((s, MLA_HEADS * V_PAD), BF16),
        ],
        scratch_shapes=[
            pltpu.VMEM((bm, Q_LORA), BF16),
            pltpu.VMEM((bm, KV_LORA), BF16),
            pltpu.VMEM((bm, LANES), BF16),
        ],
        compiler_params=_params("parallel", "arbitrary"),
        name="mla_prep",
    )(proj, proj, proj, gq, gkv, cos, sin, wq, wkv)


def _normalize(acc):
    return acc[:, :V_HEAD] / acc[:, V_HEAD:V_HEAD + 1]


def _mla_attn_kernel(q_ref, k_ref, v_ref, o_ref, *, tk):
    q = q_ref[...]
    m = acc = None
    for c in range(k_ref.shape[0] // tk):
        rows = slice(c * tk, (c + 1) * tk)
        s = lax.dot_general(q, k_ref[rows, :], _NT, preferred_element_type=F32)
        m_c = jnp.max(s, axis=-1, keepdims=True)
        if c == 0:
            m = m_c
            acc = jnp.dot(jnp.exp2(s - m).astype(BF16), v_ref[rows, :], preferred_element_type=F32)
        else:
            m_new = jnp.maximum(m, m_c)
            pv = jnp.dot(jnp.exp2(s - m_new).astype(BF16), v_ref[rows, :], preferred_element_type=F32)
            acc = jnp.exp2(m - m_new) * acc + pv
            m = m_new
    o_ref[...] = _normalize(acc).astype(o_ref.dtype)


def _mla_attn(q, k, v, tq=512, tk=2048):
    s = q.shape[0]
    kern = functools.partial(_mla_attn_kernel, tk=tk)
    return pl.pallas_call(
        kern,
        grid=(MLA_HEADS, s // tq),
        in_specs=[
            pl.BlockSpec((tq, QK_PAD), lambda h, i: (i, h)),
            pl.BlockSpec((s, QK_PAD), lambda h, i: (0, h)),
            pl.BlockSpec((s, V_PAD), lambda h, i: (0, h)),
        ],
        out_specs=pl.BlockSpec((tq, V_HEAD), lambda h, i: (i, h)),
        out_shape=jax.ShapeDtypeStruct((s, MLA_WIDTH), BF16),
        compiler_params=_params("parallel", "arbitrary"),
        name="mla_attn",
    )(q, k, v)


DIL_T = 256
DIL_SUB = 4
DIL_REACH = max(w // 2 for w, _ in DIL_PATTERNS)
DIL_SIDE = DIL_REACH // DIL_T
DIL_NCHUNK = 2 * DIL_SIDE + 1


def _dil_bias(r, slope):
    row = lax.broadcasted_iota(jnp.int32, (DIL_T, DIL_T), 0)
    col = lax.broadcasted_iota(jnp.int32, (DIL_T, DIL_T), 1)
    d = (r - DIL_SIDE) * DIL_T + col - row
    ad = jnp.abs(d)
    cnt = jnp.zeros((DIL_T, DIL_T), F32)
    for window, dil in DIL_PATTERNS:
        member = ((d & (dil - 1)) == 0) & (ad <= window // 2)
        cnt = cnt + member.astype(F32)
    bias = jnp.log2(jnp.maximum(cnt, 1.0)) - (slope * LOG2E) * ad.astype(F32)
    return jnp.where(cnt > 0.0, bias, MASK_BIAS)


def _dil_attn_kernel(slope_ref, q_ref, k_ref, v_ref, o_ref, b_ref, vp_ref, *, nblk):
    h = pl.program_id(0)
    i = pl.program_id(1)

    @pl.when(i == 0)
    def _():
        slope = slope_ref[h]
        for r in range(DIL_NCHUNK):
            b_ref[r] = _dil_bias(r, slope)
        b_ref[DIL_NCHUNK] = jnp.full((DIL_T, DIL_T), MASK_BIAS, F32)
        vp_ref[:, :DIL_HEAD_DIM] = v_ref[...]
        vp_ref[:, DIL_HEAD_DIM:] = jnp.ones((vp_ref.shape[0], V_PAD - DIL_HEAD_DIM), BF16)

    for u in range(DIL_SUB):
        rows = slice(u * DIL_T, (u + 1) * DIL_T)
        q = q_ref[rows, :]
        scores, starts = [], []
        for r in range(DIL_NCHUNK):
            kc = i * DIL_SUB + u + (r - DIL_SIDE)
            inside = (kc >= 0) & (kc < nblk)
            ks = pl.multiple_of(jnp.clip(kc, 0, nblk - 1) * DIL_T, DIL_T)
            bias = b_ref[jnp.where(inside, r, DIL_NCHUNK)]
            scores.append(lax.dot_general(q, k_ref[pl.ds(ks, DIL_T), :], _NT, preferred_element_type=F32) + bias)
            starts.append(ks)
        m = functools.reduce(jnp.maximum, [jnp.max(s, axis=-1, keepdims=True) for s in scores])
        acc = None
        for s, ks in zip(scores, starts):
            pv = jnp.dot(jnp.exp2(s - m).astype(BF16), vp_ref[pl.ds(ks, DIL_T), :], preferred_element_type=F32)
            acc = pv if acc is None else acc + pv
        o_ref[rows, :] = _normalize(acc).astype(o_ref.dtype)


def _dil_attn(proj, slopes):
    s = proj.shape[0]
    nblk = s // DIL_T
    kern = functools.partial(_dil_attn_kernel, nblk=nblk)
    hd = DIL_HEAD_DIM
    return pl.pallas_call(
        kern,
        grid=(DIL_HEADS, nblk // DIL_SUB),
        in_specs=[
            pl.BlockSpec(memory_space=pltpu.SMEM),
            pl.BlockSpec((DIL_T * DIL_SUB, hd), lambda h, i: (i, COL_QB // hd + h)),
            pl.BlockSpec((s, hd), lambda h, i: (0, COL_KB // hd + h)),
            pl.BlockSpec((s, hd), lambda h, i: (0, COL_VB // hd + h)),
        ],
        out_specs=pl.BlockSpec((DIL_T * DIL_SUB, hd), lambda h, i: (i, h)),
        out_shape=jax.ShapeDtypeStruct((s, DIL_WIDTH), BF16),
        scratch_shapes=[
            pltpu.VMEM((DIL_NCHUNK + 1, DIL_T, DIL_T), F32),
            pltpu.VMEM((s, V_PAD), BF16),
        ],
        compiler_params=_params("arbitrary", "arbitrary"),
        name="dil_attn",
    )(slopes, proj, proj, proj)


def _gated_norm(a_ref, gate_ref, g_ref, y_ref):
    for r in range(0, a_ref.shape[0], ROW_CHUNK):
        rows = slice(r, r + ROW_CHUNK)
        gate = gate_ref[rows, :].astype(F32)
        y = _rms(a_ref[rows, :].astype(F32), g_ref[...]) * (gate * jax.nn.sigmoid(gate))
        y_ref[rows, :] = y.astype(y_ref.dtype)


def _tail_kernel(aa_ref, ab_ref, ga_ref, gb_ref, gma_ref, gmb_ref, x_ref, w_ref, gple_ref, p_ref, wp_ref,
                 gfin_ref, o_ref, yh_ref, *, nj, bn, last_layer):
    jj = pl.program_id(1)

    @pl.when(jj == 0)
    def _():
        _gated_norm(aa_ref, ga_ref, gma_ref, yh_ref.at[:, :MLA_WIDTH])
        _gated_norm(ab_ref, gb_ref, gmb_ref, yh_ref.at[:, MLA_WIDTH:])

    @pl.when(jj < nj)
    def _():
        cols = pl.ds(pl.multiple_of(jj * bn, bn), bn)
        o_ref[:, cols] = x_ref[...] + jnp.dot(yh_ref[...], w_ref[...], preferred_element_type=F32)

    @pl.when(jj == nj)
    def _():
        _rms_rows(o_ref, gple_ref, yh_ref)

    @pl.when(jj >= nj)
    def _():
        cols = pl.ds(pl.multiple_of((jj - nj) * bn, bn), bn)
        gate = jax.nn.sigmoid(jnp.dot(yh_ref[...], w_ref[...], preferred_element_type=F32))
        ple = jnp.dot(p_ref[...].astype(BF16), wp_ref[...], preferred_element_type=F32)
        o_ref[:, cols] = o_ref[:, cols] + ple * gate

    if last_layer:
        @pl.when(jj == 2 * nj - 1)
        def _():
            _rms_rows(o_ref, gfin_ref, o_ref)


def _tail(attn_a, attn_b, proj, g_a, g_b, x, w_cat, g_ple, p, w_ple, g_fin, last_layer, bm=512, bn=512):
    s, d = x.shape
    assert d == MLA_WIDTH + DIL_WIDTH
    c = p.shape[1]
    nj = d // bn
    kern = functools.partial(_tail_kernel, nj=nj, bn=bn, last_layer=last_layer)
    first = lambda i, jj: (i, jnp.minimum(jj, nj - 1))
    second = lambda i, jj: (0, jnp.maximum(jj - nj, 0))
    row = lambda i, jj: (i, 0)
    vec = lambda i, jj: (0, 0)
    return pl.pallas_call(
        kern,
        grid=(s // bm, 2 * nj),
        in_specs=[
            pl.BlockSpec((bm, MLA_WIDTH), row),
            pl.BlockSpec((bm, DIL_WIDTH), row),
            pl.BlockSpec((bm, MLA_WIDTH), lambda i, jj: (i, COL_GATE_A // MLA_WIDTH)),
            pl.BlockSpec((bm, DIL_WIDTH), lambda i, jj: (i, COL_GATE_B // DIL_WIDTH)),
            pl.BlockSpec((1, MLA_WIDTH), vec),
            pl.BlockSpec((1, DIL_WIDTH), vec),
            pl.BlockSpec((bm, bn), first),
            pl.BlockSpec((d, bn), lambda i, jj: (jj // nj, jj % nj)),
            pl.BlockSpec((1, d), vec),
            pl.BlockSpec((bm, c), row),
            pl.BlockSpec((c, bn), second),
            pl.BlockSpec((1, d), vec),
        ],
        out_specs=pl.BlockSpec((bm, d), row),
        out_shape=jax.ShapeDtypeStruct((s, d), F32),
        scratch_shapes=[pltpu.VMEM((bm, d), BF16)],
        compiler_params=_params("parallel", "arbitrary"),
        name="tail",
    )(attn_a, attn_b, proj, proj, g_a, g_b, x, w_cat, g_ple, p, w_ple, g_fin)


def _rope_dup(w, axis=-1):
    t1, t2 = jnp.split(w, 2, axis=axis)
    return jnp.concatenate([t1, t2, t2, t1], axis=axis)


def _prep_w_in(w):
    wt = w.T
    d = wt.shape[1]
    lat = Q_LORA + KV_LORA
    head = lat + QK_ROPE
    row_scale = np.ones((wt.shape[0] - head, 1), np.float32)
    row_scale[COL_QB:COL_KB] = DIL_HEAD_DIM ** -0.5 * LOG2E
    w_main = (wt[head:] * row_scale).astype(BF16)
    small = lax.optimization_barrier(wt[:head])
    pad = jnp.zeros((PROJ_WIDTH - COL_KR - LANES, d), wt.dtype)
    w_tail = jnp.concatenate([small[:lat], _rope_dup(small[lat:], axis=0), pad], axis=0).astype(BF16)
    return w_main, w_tail


def _prep_w_uq(w):
    r = w.shape[0]
    w = w.reshape(r, MLA_HEADS, QK_NOPE + QK_ROPE) * ((QK_NOPE + QK_ROPE) ** -0.5 * LOG2E)
    w = jnp.concatenate([w[..., :QK_NOPE], _rope_dup(w[..., QK_NOPE:])], axis=-1)
    return w.reshape(r, MLA_HEADS * QK_PAD).astype(BF16)


def _rope_tables(positions):
    inv = ROPE_THETA ** (-jnp.arange(ROPE_HALF, dtype=F32) / ROPE_HALF)
    ang = positions.astype(F32)[:, None] * inv
    cos, sin = jnp.cos(ang), jnp.sin(ang)
    zero = jnp.zeros_like(cos)
    return (jnp.concatenate([cos, cos, zero, zero], axis=1),
            jnp.concatenate([-sin, sin, zero, zero], axis=1))


def kernel(x, p, positions, g_mix, w_in, g_q_latent, w_uq, g_kv_latent, w_ukv, g_out_mla, g_out_dil,
           w_out, w_ple, g_ple, w_ple_gate, g_final):
    b, s, d = x.shape
    assert b == 1 and s % (DIL_T * DIL_SUB) == 0
    slopes = jnp.asarray(2.0 ** (-ALIBI_MAX_BIAS * np.arange(1, DIL_HEADS + 1) / DIL_HEADS), F32)
    x2d = x.reshape(s, d)
    cos, sin = _rope_tables(positions[0])
    depth = g_mix.shape[0]
    for i in range(depth):
        proj = _in_proj(x2d, g_mix[i][None], *_prep_w_in(w_in[i]))
        q_a, k_a, v_a = _mla_prep(proj, g_q_latent[i][None], g_kv_latent[i][None], cos, sin,
                                  _prep_w_uq(w_uq[i]), w_ukv[i].astype(BF16))
        attn_a = _mla_attn(q_a, k_a, v_a)
        attn_b = _dil_attn(proj, slopes)
        w_cat = jnp.concatenate([w_out[i], w_ple_gate[i]], axis=0).astype(BF16)
        x2d = _tail(attn_a, attn_b, proj, g_out_mla[i][None], g_out_dil[i][None], x2d, w_cat,
                    g_ple[i][None], p[i, 0], w_ple[i].astype(BF16), g_final[None], i == depth - 1)
    return x2d.reshape(b, s, d)
```

```python
import functools

import numpy as np
import jax
import jax.numpy as jnp
from jax import lax
from jax.experimental import pallas as pl
from jax.experimental.pallas import tpu as pltpu

F32 = jnp.float32
BF16 = jnp.bfloat16

EPS = 1e-6
ROPE_THETA = 10000.0
ALIBI_MAX_BIAS = 8.0

MLA_HEADS = 16
Q_LORA = 1024
KV_LORA = 512
QK_NOPE = 128
QK_ROPE = 64
V_HEAD = 128
MLA_WIDTH = MLA_HEADS * V_HEAD
DIL_HEADS = 16
DIL_HEAD_DIM = 128
DIL_WIDTH = DIL_HEADS * DIL_HEAD_DIM
DIL_PATTERNS = ((128, 1), (512, 4), (2048, 16))

LANES = 128
QK_PAD = 256
V_PAD = 2 * LANES
ROPE_HALF = QK_ROPE // 2
LOG2E = float(np.log2(np.e))

COL_GATE_A = 0
COL_QB = COL_GATE_A + MLA_WIDTH
COL_KB = COL_QB + DIL_WIDTH
COL_VB = COL_KB + DIL_WIDTH
COL_GATE_B = COL_VB + DIL_WIDTH
COL_CQ = COL_GATE_B + DIL_WIDTH
COL_CKV = COL_CQ + Q_LORA
COL_KR = COL_CKV + KV_LORA
PROJ_WIDTH = 12288

MASK_BIAS = -1e30

ROW_CHUNK = 64

VMEM_LIMIT = 60 * 1024 * 1024


def _params(*sem):
    return pltpu.CompilerParams(dimension_semantics=sem, vmem_limit_bytes=VMEM_LIMIT)


def _rms(x, g):
    return x * lax.rsqrt(jnp.mean(x * x, axis=-1, keepdims=True) + EPS) * g


def _rms_rows(x_ref, g_ref, h_ref):
    for r in range(0, x_ref.shape[0], ROW_CHUNK):
        rows = slice(r, r + ROW_CHUNK)
        h_ref[rows, :] = _rms(x_ref[rows, :].astype(F32), g_ref[...]).astype(h_ref.dtype)


_NT = (((1,), (1,)), ((), ()))


def _in_proj_kernel(x_ref, g_ref, wm_ref, wt_ref, o_ref, h_ref, *, n_main):
    j = pl.program_id(1)

    @pl.when(j == 0)
    def _():
        _rms_rows(x_ref, g_ref, h_ref)

    @pl.when(j < n_main)
    def _():
        o_ref[...] = lax.dot_general(h_ref[...], wm_ref[...], _NT, preferred_element_type=F32).astype(o_ref.dtype)

    @pl.when(j >= n_main)
    def _():
        o_ref[...] = lax.dot_general(h_ref[...], wt_ref[...], _NT, preferred_element_type=F32).astype(o_ref.dtype)


def _in_proj(x, g, w_main, w_tail, bm=512, bn=1024):
    s, d = x.shape
    n_main = w_main.shape[0] // bn
    n_tail = w_tail.shape[0] // bn
    kern = functools.partial(_in_proj_kernel, n_main=n_main)
    return pl.pallas_call(
        kern,
        grid=(s // bm, n_main + n_tail),
        in_specs=[
            pl.BlockSpec((bm, d), lambda i, j: (i, 0)),
            pl.BlockSpec((1, d), lambda i, j: (0, 0)),
            pl.BlockSpec((bn, d), lambda i, j: (jnp.minimum(j, n_main - 1), 0)),
            pl.BlockSpec((bn, d), lambda i, j: (jnp.maximum(j - n_main, 0), 0)),
        ],
        out_specs=pl.BlockSpec((bm, bn), lambda i, j: (i, j)),
        out_shape=jax.ShapeDtypeStruct((s, (n_main + n_tail) * bn), BF16),
        scratch_shapes=[pltpu.VMEM((bm, d), BF16)],
        compiler_params=_params("parallel", "arbitrary"),
        name="in_proj",
    )(x, g, w_main, w_tail)


def _rope(t, cos, sin):
    return t * cos + pltpu.roll(t, 2 * ROPE_HALF, axis=1) * sin


def _mla_prep_kernel(cq_ref, ckv_ref, kr_ref, gq_ref, gkv_ref, cos_ref, sin_ref, wq_ref, wkv_ref,
                     q_out, k_out, v_out, cqn_ref, ckvn_ref, kpe_ref, *, heads):
    @pl.when(pl.program_id(1) == 0)
    def _():
        cqn_ref[...] = _rms(cq_ref[...].astype(F32), gq_ref[...]).astype(BF16)
        ckvn_ref[...] = _rms(ckv_ref[...].astype(F32), gkv_ref[...]).astype(BF16)
        kpe_ref[...] = _rope(kr_ref[...].astype(F32), cos_ref[...], sin_ref[...]).astype(BF16)

    q = jnp.dot(cqn_ref[...], wq_ref[...], preferred_element_type=F32)
    kv = jnp.dot(ckvn_ref[...], wkv_ref[...], preferred_element_type=F32)
    cos = cos_ref[...]
    sin = sin_ref[...]
    for g in range(heads):
        lo = g * QK_PAD
        mid = lo + LANES
        hi = lo + QK_PAD
        q_out[:, lo:mid] = q[:, lo:mid].astype(BF16)
        q_out[:, mid:hi] = _rope(q[:, mid:hi], cos, sin).astype(BF16)
        k_out[:, lo:mid] = kv[:, lo:mid].astype(BF16)
        k_out[:, mid:hi] = kpe_ref[...]
        v_out[:, lo:mid] = kv[:, mid:hi].astype(BF16)
        v_out[:, mid:hi] = jnp.ones((v_out.shape[0], LANES), BF16)


def _mla_prep(proj, gq, gkv, cos, sin, wq, wkv, bm=512, heads=4):
    s = proj.shape[0]
    bq = heads * QK_PAD
    kern = functools.partial(_mla_prep_kernel, heads=heads)
    return pl.pallas_call(
        kern,
        grid=(s // bm, MLA_HEADS // heads),
        in_specs=[
            pl.BlockSpec((bm, Q_LORA), lambda i, j: (i, COL_CQ // Q_LORA)),
            pl.BlockSpec((bm, KV_LORA), lambda i, j: (i, COL_CKV // KV_LORA)),
            pl.BlockSpec((bm, LANES), lambda i, j: (i, COL_KR // LANES)),
            pl.BlockSpec((1, Q_LORA), lambda i, j: (0, 0)),
            pl.BlockSpec((1, KV_LORA), lambda i, j: (0, 0)),
            pl.BlockSpec((bm, LANES), lambda i, j: (i, 0)),
            pl.BlockSpec((bm, LANES), lambda i, j: (i, 0)),
            pl.BlockSpec((Q_LORA, bq), lambda i, j: (0, j)),
            pl.BlockSpec((KV_LORA, bq), lambda i, j: (0, j)),
        ],
        out_specs=[
            pl.BlockSpec((bm, bq), lambda i, j: (i, j)),
            pl.BlockSpec((bm, bq), lambda i, j: (i, j)),
            pl.BlockSpec((bm, heads * V_PAD), lambda i, j: (i, j)),
        ],
        out_shape=[
            jax.ShapeDtypeStruct((s, MLA_HEADS * QK_PAD), BF16),
            jax.ShapeDtypeStruct((s, MLA_HEADS * QK_PAD), BF16),
            jax.ShapeDtypeStruct((s, MLA_HEADS * V_PAD), BF16),
        ],
        scratch_shapes=[
            pltpu.VMEM((bm, Q_LORA), BF16),
            pltpu.VMEM((bm, KV_LORA), BF16),
            pltpu.VMEM((bm, LANES), BF16),
        ],
        compiler_params=_params("parallel", "arbitrary"),
        name="mla_prep",
    )(proj, proj, proj, gq, gkv, cos, sin, wq, wkv)


def _normalize(acc):
    return acc[:, :V_HEAD] / acc[:, V_HEAD:V_HEAD + 1]


def _mla_attn_kernel(q_ref, k_ref, v_ref, o_ref, *, tq, tk):
    for u in range(q_ref.shape[0] // tq):
        q_rows = slice(u * tq, (u + 1) * tq)
        q = q_ref[q_rows, :]
        m = acc = None
        for c in range(k_ref.shape[0] // tk):
            rows = slice(c * tk, (c + 1) * tk)
            s = lax.dot_general(q, k_ref[rows, :], _NT, preferred_element_type=F32)
            m_c = jnp.max(s, axis=-1, keepdims=True)
            if c == 0:
                m = m_c
                acc = jnp.dot(jnp.exp2(s - m).astype(BF16), v_ref[rows, :], preferred_element_type=F32)
            else:
                m_new = jnp.maximum(m, m_c)
                pv = jnp.dot(jnp.exp2(s - m_new).astype(BF16), v_ref[rows, :], preferred_element_type=F32)
                acc = jnp.exp2(m - m_new) * acc + pv
                m = m_new
        o_ref[q_rows, :] = _normalize(acc).astype(o_ref.dtype)


def _mla_attn(q, k, v, tq=512, sub=2, tk=1024):
    s = q.shape[0]
    kern = functools.partial(_mla_attn_kernel, tq=tq, tk=tk)
    return pl.pallas_call(
        kern,
        grid=(MLA_HEADS, s // (tq * sub)),
        in_specs=[
            pl.BlockSpec((tq * sub, QK_PAD), lambda h, i: (i, h)),
            pl.BlockSpec((s, QK_PAD), lambda h, i: (0, h)),
            pl.BlockSpec((s, V_PAD), lambda h, i: (0, h)),
        ],
        out_specs=pl.BlockSpec((tq * sub, V_HEAD), lambda h, i: (i, h)),
        out_shape=jax.ShapeDtypeStruct((s, MLA_WIDTH), BF16),
        compiler_params=_params("parallel", "arbitrary"),
        name="mla_attn",
    )(q, k, v)


DIL_T = 256
DIL_SUB = 4
DIL_REACH = max(w // 2 for w, _ in DIL_PATTERNS)
DIL_SIDE = DIL_REACH // DIL_T
DIL_NCHUNK = 2 * DIL_SIDE + 1


def _dil_bias(r, slope):
    row = lax.broadcasted_iota(jnp.int32, (DIL_T, DIL_T), 0)
    col = lax.broadcasted_iota(jnp.int32, (DIL_T, DIL_T), 1)
    d = (r - DIL_SIDE) * DIL_T + col - row
    ad = jnp.abs(d)
    cnt = jnp.zeros((DIL_T, DIL_T), F32)
    for window, dil in DIL_PATTERNS:
        member = ((d & (dil - 1)) == 0) & (ad <= window // 2)
        cnt = cnt + member.astype(F32)
    bias = jnp.log2(jnp.maximum(cnt, 1.0)) - (slope * LOG2E) * ad.astype(F32)
    return jnp.where(cnt > 0.0, bias, MASK_BIAS)


def _dil_attn_kernel(slope_ref, q_ref, k_ref, v_ref, o_ref, b_ref, vp_ref, *, nblk):
    h = pl.program_id(0)
    i = pl.program_id(1)

    @pl.when(i == 0)
    def _():
        slope = slope_ref[h]
        for r in range(DIL_NCHUNK):
            b_ref[r] = _dil_bias(r, slope)
        b_ref[DIL_NCHUNK] = jnp.full((DIL_T, DIL_T), MASK_BIAS, F32)
        vp_ref[:, :DIL_HEAD_DIM] = v_ref[...]
        vp_ref[:, DIL_HEAD_DIM:] = jnp.ones((vp_ref.shape[0], V_PAD - DIL_HEAD_DIM), BF16)

    for u in range(DIL_SUB):
        rows = slice(u * DIL_T, (u + 1) * DIL_T)
        q = q_ref[rows, :]
        scores, starts = [], []
        for r in range(DIL_NCHUNK):
            kc = i * DIL_SUB + u + (r - DIL_SIDE)
            inside = (kc >= 0) & (kc < nblk)
            ks = pl.multiple_of(jnp.clip(kc, 0, nblk - 1) * DIL_T, DIL_T)
            bias = b_ref[jnp.where(inside, r, DIL_NCHUNK)]
            scores.append(lax.dot_general(q, k_ref[pl.ds(ks, DIL_T), :], _NT, preferred_element_type=F32) + bias)
            starts.append(ks)
        m = functools.reduce(jnp.maximum, [jnp.max(s, axis=-1, keepdims=True) for s in scores])
        acc = None
        for s, ks in zip(scores, starts):
            pv = jnp.dot(jnp.exp2(s - m).astype(BF16), vp_ref[pl.ds(ks, DIL_T), :], preferred_element_type=F32)
            acc = pv if acc is None else acc + pv
        o_ref[rows, :] = _normalize(acc).astype(o_ref.dtype)


def _dil_attn(proj, slopes):
    s = proj.shape[0]
    nblk = s // DIL_T
    kern = functools.partial(_dil_attn_kernel, nblk=nblk)
    hd = DIL_HEAD_DIM
    return pl.pallas_call(
        kern,
        grid=(DIL_HEADS, nblk // DIL_SUB),
        in_specs=[
            pl.BlockSpec(memory_space=pltpu.SMEM),
            pl.BlockSpec((DIL_T * DIL_SUB, hd), lambda h, i: (i, COL_QB // hd + h)),
            pl.BlockSpec((s, hd), lambda h, i: (0, COL_KB // hd + h)),
            pl.BlockSpec((s, hd), lambda h, i: (0, COL_VB // hd + h)),
        ],
        out_specs=pl.BlockSpec((DIL_T * DIL_SUB, hd), lambda h, i: (i, h)),
        out_shape=jax.ShapeDtypeStruct((s, DIL_WIDTH), BF16),
        scratch_shapes=[
            pltpu.VMEM((DIL_NCHUNK + 1, DIL_T, DIL_T), F32),
            pltpu.VMEM((s, V_PAD), BF16),
        ],
        compiler_params=_params("arbitrary", "arbitrary"),
        name="dil_attn",
    )(slopes, proj, proj, proj)


def _gated_norm(a_ref, gate_ref, g_ref, y_ref):
    for r in range(0, a_ref.shape[0], ROW_CHUNK):
        rows = slice(r, r + ROW_CHUNK)
        gate = gate_ref[rows, :].astype(F32)
        y = _rms(a_ref[rows, :].astype(F32), g_ref[...]) * (gate * jax.nn.sigmoid(gate))
        y_ref[rows, :] = y.astype(y_ref.dtype)


def _tail_kernel(aa_ref, ab_ref, ga_ref, gb_ref, gma_ref, gmb_ref, x_ref, w_ref, gple_ref, p_ref, wp_ref,
                 gfin_ref, o_ref, yh_ref, *, nj, bn, last_layer):
    jj = pl.program_id(1)

    @pl.when(jj == 0)
    def _():
        _gated_norm(aa_ref, ga_ref, gma_ref, yh_ref.at[:, :MLA_WIDTH])
        _gated_norm(ab_ref, gb_ref, gmb_ref, yh_ref.at[:, MLA_WIDTH:])

    @pl.when(jj < nj)
    def _():
        cols = pl.ds(pl.multiple_of(jj * bn, bn), bn)
        o_ref[:, cols] = x_ref[...] + jnp.dot(yh_ref[...], w_ref[...], preferred_element_type=F32)

    @pl.when(jj == nj)
    def _():
        _rms_rows(o_ref, gple_ref, yh_ref)

    @pl.when(jj >= nj)
    def _():
        cols = pl.ds(pl.multiple_of((jj - nj) * bn, bn), bn)
        gate = jax.nn.sigmoid(jnp.dot(yh_ref[...], w_ref[...], preferred_element_type=F32))
        ple = jnp.dot(p_ref[...].astype(BF16), wp_ref[...], preferred_element_type=F32)
        o_ref[:, cols] = o_ref[:, cols] + ple * gate

    if last_layer:
        @pl.when(jj == 2 * nj - 1)
        def _():
            _rms_rows(o_ref, gfin_ref, o_ref)


def _tail(attn_a, attn_b, proj, g_a, g_b, x, w_cat, g_ple, p, w_ple, g_fin, last_layer, bm=512, bn=512):
    s, d = x.shape
    assert d == MLA_WIDTH + DIL_WIDTH
    c = p.shape[1]
    nj = d // bn
    kern = functools.partial(_tail_kernel, nj=nj, bn=bn, last_layer=last_layer)
    first = lambda i, jj: (i, jnp.minimum(jj, nj - 1))
    second = lambda i, jj: (0, jnp.maximum(jj - nj, 0))
    row = lambda i, jj: (i, 0)
    vec = lambda i, jj: (0, 0)
    return pl.pallas_call(
        kern,
        grid=(s // bm, 2 * nj),
        in_specs=[
            pl.BlockSpec((bm, MLA_WIDTH), row),
            pl.BlockSpec((bm, DIL_WIDTH), row),
            pl.BlockSpec((bm, MLA_WIDTH), lambda i, jj: (i, COL_GATE_A // MLA_WIDTH)),
            pl.BlockSpec((bm, DIL_WIDTH), lambda i, jj: (i, COL_GATE_B // DIL_WIDTH)),
            pl.BlockSpec((1, MLA_WIDTH), vec),
            pl.BlockSpec((1, DIL_WIDTH), vec),
            pl.BlockSpec((bm, bn), first),
            pl.BlockSpec((d, bn), lambda i, jj: (jj // nj, jj % nj)),
            pl.BlockSpec((1, d), vec),
            pl.BlockSpec((bm, c), row),
            pl.BlockSpec((c, bn), second),
            pl.BlockSpec((1, d), vec),
        ],
        out_specs=pl.BlockSpec((bm, d), row),
        out_shape=jax.ShapeDtypeStruct((s, d), F32),
        scratch_shapes=[pltpu.VMEM((bm, d), BF16)],
        compiler_params=_params("parallel", "arbitrary"),
        name="tail",
    )(attn_a, attn_b, proj, proj, g_a, g_b, x, w_cat, g_ple, p, w_ple, g_fin)


def _rope_dup(w, axis=-1):
    t1, t2 = jnp.split(w, 2, axis=axis)
    return jnp.concatenate([t1, t2, t2, t1], axis=axis)


def _prep_w_in(w):
    wt = w.T
    d = wt.shape[1]
    lat = Q_LORA + KV_LORA
    head = lat + QK_ROPE
    row_scale = np.ones((wt.shape[0] - head, 1), np.float32)
    row_scale[COL_QB:COL_KB] = DIL_HEAD_DIM ** -0.5 * LOG2E
    w_main = (wt[head:] * row_scale).astype(BF16)
    small = lax.optimization_barrier(wt[:head])
    pad = jnp.zeros((PROJ_WIDTH - COL_KR - LANES, d), wt.dtype)
    w_tail = jnp.concatenate([small[:lat], _rope_dup(small[lat:], axis=0), pad], axis=0).astype(BF16)
    return w_main, w_tail


def _prep_w_uq(w):
    r = w.shape[0]
    w = w.reshape(r, MLA_HEADS, QK_NOPE + QK_ROPE) * ((QK_NOPE + QK_ROPE) ** -0.5 * LOG2E)
    w = jnp.concatenate([w[..., :QK_NOPE], _rope_dup(w[..., QK_NOPE:])], axis=-1)
    return w.reshape(r, MLA_HEADS * QK_PAD).astype(BF16)


def _rope_tables(positions):
    inv = ROPE_THETA ** (-jnp.arange(ROPE_HALF, dtype=F32) / ROPE_HALF)
    ang = positions.astype(F32)[:, None] * inv
    cos, sin = jnp.cos(ang), jnp.sin(ang)
    zero = jnp.zeros_like(cos)
    return (jnp.concatenate([cos, cos, zero, zero], axis=1),
            jnp.concatenate([-sin, sin, zero, zero], axis=1))


def kernel(x, p, positions, g_mix, w_in, g_q_latent, w_uq, g_kv_latent, w_ukv, g_out_mla, g_out_dil,
           w_out, w_ple, g_ple, w_ple_gate, g_final):
    b, s, d = x.shape
    assert b == 1 and s % (DIL_T * DIL_SUB) == 0
    slopes = jnp.asarray(2.0 ** (-ALIBI_MAX_BIAS * np.arange(1, DIL_HEADS + 1) / DIL_HEADS), F32)
    x2d = x.reshape(s, d)
    cos, sin = _rope_tables(positions[0])
    depth = g_mix.shape[0]
    for i in range(depth):
        proj = _in_proj(x2d, g_mix[i][None], *_prep_w_in(w_in[i]))
        q_a, k_a, v_a = _mla_prep(proj, g_q_latent[i][None], g_kv_latent[i][None], cos, sin,
                                  _prep_w_uq(w_uq[i]), w_ukv[i].astype(BF16))
        attn_a = _mla_attn(q_a, k_a, v_a)
        attn_b = _dil_attn(proj, slopes)
        w_cat = lax.dynamic_update_slice(jnp.zeros((2 * d, d), BF16), w_out[i].astype(BF16), (0, 0))
        w_cat = lax.dynamic_update_slice(w_cat, w_ple_gate[i].astype(BF16), (d, 0))
        x2d = _tail(attn_a, attn_b, proj, g_out_mla[i][None], g_out_dil[i][None], x2d, w_cat,
                    g_ple[i][None], p[i, 0], w_ple[i].astype(BF16), g_final[None], i == depth - 1)
    return x2d.reshape(b, s, d)
```

```python
import functools

import numpy as np
import jax
import jax.numpy as jnp
from jax import lax
from jax.experimental import pallas as pl
from jax.experimental.pallas import tpu as pltpu

F32 = jnp.float32
BF16 = jnp.bfloat16

EPS = 1e-6
ROPE_THETA = 10000.0
ALIBI_MAX_BIAS = 8.0

MLA_HEADS = 16
Q_LORA = 1024
KV_LORA = 512
QK_NOPE = 128
QK_ROPE = 64
V_HEAD = 128
MLA_WIDTH = MLA_HEADS * V_HEAD
DIL_HEADS = 16
DIL_HEAD_DIM = 128
DIL_WIDTH = DIL_HEADS * DIL_HEAD_DIM
DIL_PATTERNS = ((128, 1), (512, 4), (2048, 16))

LANES = 128
QK_PAD = 256
V_PAD = 2 * LANES
ROPE_HALF = QK_ROPE // 2
LOG2E = float(np.log2(np.e))

COL_GATE_A = 0
COL_QB = COL_GATE_A + MLA_WIDTH
COL_KB = COL_QB + DIL_WIDTH
COL_VB = COL_KB + DIL_WIDTH
COL_GATE_B = COL_VB + DIL_WIDTH
COL_CQ = COL_GATE_B + DIL_WIDTH
COL_CKV = COL_CQ + Q_LORA
COL_KR = COL_CKV + KV_LORA
PROJ_WIDTH = 12288

MASK_BIAS = -1e30

ROW_CHUNK = 64

VMEM_LIMIT = 60 * 1024 * 1024


def _params(*sem):
    return pltpu.CompilerParams(dimension_semantics=sem, vmem_limit_bytes=VMEM_LIMIT)


def _rms(x, g):
    return x * lax.rsqrt(jnp.mean(x * x, axis=-1, keepdims=True) + EPS) * g


def _rms_rows(x_ref, g_ref, h_ref):
    for r in range(0, x_ref.shape[0], ROW_CHUNK):
        rows = slice(r, r + ROW_CHUNK)
        h_ref[rows, :] = _rms(x_ref[rows, :].astype(F32), g_ref[...]).astype(h_ref.dtype)


_NT = (((1,), (1,)), ((), ()))


def _in_proj_kernel(x_ref, g_ref, wm_ref, wt_ref, o_ref, h_ref, *, n_main):
    j = pl.program_id(1)

    @pl.when(j == 0)
    def _():
        _rms_rows(x_ref, g_ref, h_ref)

    @pl.when(j < n_main)
    def _():
        o_ref[...] = lax.dot_general(h_ref[...], wm_ref[...], _NT, preferred_element_type=F32).astype(o_ref.dtype)

    @pl.when(j >= n_main)
    def _():
        o_ref[...] = lax.dot_general(h_ref[...], wt_ref[...], _NT, preferred_element_type=F32).astype(o_ref.dtype)


def _in_proj(x, g, w_main, w_tail, bm=512, bn=1024):
    s, d = x.shape
    n_main = w_main.shape[0] // bn
    n_tail = w_tail.shape[0] // bn
    kern = functools.partial(_in_proj_kernel, n_main=n_main)
    return pl.pallas_call(
        kern,
        grid=(s // bm, n_main + n_tail),
        in_specs=[
            pl.BlockSpec((bm, d), lambda i, j: (i, 0)),
            pl.BlockSpec((1, d), lambda i, j: (0, 0)),
            pl.BlockSpec((bn, d), lambda i, j: (jnp.minimum(j, n_main - 1), 0)),
            pl.BlockSpec((bn, d), lambda i, j: (jnp.maximum(j - n_main, 0), 0)),
        ],
        out_specs=pl.BlockSpec((bm, bn), lambda i, j: (i, j)),
        out_shape=jax.ShapeDtypeStruct((s, (n_main + n_tail) * bn), BF16),
        scratch_shapes=[pltpu.VMEM((bm, d), BF16)],
        compiler_params=_params("parallel", "arbitrary"),
        name="in_proj",
    )(x, g, w_main, w_tail)


def _rope(t, cos, sin):
    return t * cos + pltpu.roll(t, 2 * ROPE_HALF, axis=1) * sin


def _mla_prep_kernel(cq_ref, ckv_ref, kr_ref, gq_ref, gkv_ref, cos_ref, sin_ref, wq_ref, wkv_ref,
                     q_out, k_out, v_out, cqn_ref, ckvn_ref, kpe_ref, *, heads):
    @pl.when(pl.program_id(1) == 0)
    def _():
        cqn_ref[...] = _rms(cq_ref[...].astype(F32), gq_ref[...]).astype(BF16)
        ckvn_ref[...] = _rms(ckv_ref[...].astype(F32), gkv_ref[...]).astype(BF16)
        kpe_ref[...] = _rope(kr_ref[...].astype(F32), cos_ref[...], sin_ref[...]).astype(BF16)

    q = jnp.dot(cqn_ref[...], wq_ref[...], preferred_element_type=F32)
    kv = jnp.dot(ckvn_ref[...], wkv_ref[...], preferred_element_type=F32)
    cos = cos_ref[...]
    sin = sin_ref[...]
    for g in range(heads):
        lo = g * QK_PAD
        mid = lo + LANES
        hi = lo + QK_PAD
        q_out[:, lo:mid] = q[:, lo:mid].astype(BF16)
        q_out[:, mid:hi] = _rope(q[:, mid:hi], cos, sin).astype(BF16)
        k_out[:, lo:mid] = kv[:, lo:mid].astype(BF16)
        k_out[:, mid:hi] = kpe_ref[...]
        v_out[:, lo:mid] = kv[:, mid:hi].astype(BF16)
        v_out[:, mid:hi] = jnp.ones((v_out.shape[0], LANES), BF16)


def _mla_prep(proj, gq, gkv, cos, sin, wq, wkv, bm=512, heads=8):
    s = proj.shape[0]
    bq = heads * QK_PAD
    kern = functools.partial(_mla_prep_kernel, heads=heads)
    return pl.pallas_call(
        kern,
        grid=(s // bm, MLA_HEADS // heads),
        in_specs=[
            pl.BlockSpec((bm, Q_LORA), lambda i, j: (i, COL_CQ // Q_LORA)),
            pl.BlockSpec((bm, KV_LORA), lambda i, j: (i, COL_CKV // KV_LORA)),
            pl.BlockSpec((bm, LANES), lambda i, j: (i, COL_KR // LANES)),
            pl.BlockSpec((1, Q_LORA), lambda i, j: (0, 0)),
            pl.BlockSpec((1, KV_LORA), lambda i, j: (0, 0)),
            pl.BlockSpec((bm, LANES), lambda i, j: (i, 0)),
            pl.BlockSpec((bm, LANES), lambda i, j: (i, 0)),
            pl.BlockSpec((Q_LORA, bq), lambda i, j: (0, j)),
            pl.BlockSpec((KV_LORA, bq), lambda i, j: (0, j)),
        ],
        out_specs=[
            pl.BlockSpec((bm, bq), lambda i, j: (i, j)),
            pl.BlockSpec((bm, bq), lambda i, j: (i, j)),
            pl.BlockSpec((bm, heads * V_PAD), lambda i, j: (i, j)),
        ],
        out_shape=[
            jax.ShapeDtypeStruct((s, MLA_HEADS * QK_PAD), BF16),
            jax.ShapeDtypeStruct((s, MLA_HEADS * QK_PAD), BF16),
            jax.ShapeDtypeStruct((s, MLA_HEADS * V_PAD), BF16),
        ],
        scratch_shapes=[
            pltpu.VMEM((bm, Q_LORA), BF16),
            pltpu.VMEM((bm, KV_LORA), BF16),
            pltpu.VMEM((bm, LANES), BF16),
        ],
        compiler_params=_params("parallel", "arbitrary"),
        name="mla_prep",
    )(proj, proj, proj, gq, gkv, cos, sin, wq, wkv)


def _normalize(acc):
    return acc[:, :V_HEAD] / acc[:, V_HEAD:V_HEAD + 1]


def _mla_attn_kernel(q_ref, k_ref, v_ref, o_ref, *, tq, tk):
    for u in range(q_ref.shape[0] // tq):
        q_rows = slice(u * tq, (u + 1) * tq)
        q = q_ref[q_rows, :]
        m = acc = None
        for c in range(k_ref.shape[0] // tk):
            rows = slice(c * tk, (c + 1) * tk)
            s = lax.dot_general(q, k_ref[rows, :], _NT, preferred_element_type=F32)
            m_c = jnp.max(s, axis=-1, keepdims=True)
            if c == 0:
                m = m_c
                acc = jnp.dot(jnp.exp2(s - m).astype(BF16), v_ref[rows, :], preferred_element_type=F32)
            else:
                m_new = jnp.maximum(m, m_c)
                pv = jnp.dot(jnp.exp2(s - m_new).astype(BF16), v_ref[rows, :], preferred_element_type=F32)
                acc = jnp.exp2(m - m_new) * acc + pv
                m = m_new
        o_ref[q_rows, :] = _normalize(acc).astype(o_ref.dtype)


def _mla_attn(q, k, v, tq=512, sub=2, tk=1024):
    s = q.shape[0]
    kern = functools.partial(_mla_attn_kernel, tq=tq, tk=tk)
    return pl.pallas_call(
        kern,
        grid=(MLA_HEADS, s // (tq * sub)),
        in_specs=[
            pl.BlockSpec((tq * sub, QK_PAD), lambda h, i: (i, h)),
            pl.BlockSpec((s, QK_PAD), lambda h, i: (0, h)),
            pl.BlockSpec((s, V_PAD), lambda h, i: (0, h)),
        ],
        out_specs=pl.BlockSpec((tq * sub, V_HEAD), lambda h, i: (i, h)),
        out_shape=jax.ShapeDtypeStruct((s, MLA_WIDTH), BF16),
        compiler_params=_params("parallel", "arbitrary"),
        name="mla_attn",
    )(q, k, v)


DIL_T = 256
DIL_SUB = 8
DIL_REACH = max(w // 2 for w, _ in DIL_PATTERNS)
DIL_SIDE = DIL_REACH // DIL_T
DIL_NCHUNK = 2 * DIL_SIDE + 1


def _dil_bias(r, slope):
    row = lax.broadcasted_iota(jnp.int32, (DIL_T, DIL_T), 0)
    col = lax.broadcasted_iota(jnp.int32, (DIL_T, DIL_T), 1)
    d = (r - DIL_SIDE) * DIL_T + col - row
    ad = jnp.abs(d)
    cnt = jnp.zeros((DIL_T, DIL_T), F32)
    for window, dil in DIL_PATTERNS:
        member = ((d & (dil - 1)) == 0) & (ad <= window // 2)
        cnt = cnt + member.astype(F32)
    bias = jnp.log2(jnp.maximum(cnt, 1.0)) - (slope * LOG2E) * ad.astype(F32)
    return jnp.where(cnt > 0.0, bias, MASK_BIAS)


def _dil_attn_kernel(slope_ref, q_ref, k_ref, v_ref, o_ref, b_ref, vp_ref, *, nblk):
    h = pl.program_id(0)
    i = pl.program_id(1)

    @pl.when(i == 0)
    def _():
        slope = slope_ref[h]
        for r in range(DIL_NCHUNK):
            b_ref[r] = _dil_bias(r, slope)
        b_ref[DIL_NCHUNK] = jnp.full((DIL_T, DIL_T), MASK_BIAS, F32)
        vp_ref[:, :DIL_HEAD_DIM] = v_ref[...]
        vp_ref[:, DIL_HEAD_DIM:] = jnp.ones((vp_ref.shape[0], V_PAD - DIL_HEAD_DIM), BF16)

    for u in range(DIL_SUB):
        rows = slice(u * DIL_T, (u + 1) * DIL_T)
        q = q_ref[rows, :]
        scores, starts = [], []
        for r in range(DIL_NCHUNK):
            kc = i * DIL_SUB + u + (r - DIL_SIDE)
            inside = (kc >= 0) & (kc < nblk)
            ks = pl.multiple_of(jnp.clip(kc, 0, nblk - 1) * DIL_T, DIL_T)
            bias = b_ref[jnp.where(inside, r, DIL_NCHUNK)]
            scores.append(lax.dot_general(q, k_ref[pl.ds(ks, DIL_T), :], _NT, preferred_element_type=F32) + bias)
            starts.append(ks)
        m = functools.reduce(jnp.maximum, [jnp.max(s, axis=-1, keepdims=True) for s in scores])
        acc = None
        for s, ks in zip(scores, starts):
            pv = jnp.dot(jnp.exp2(s - m).astype(BF16), vp_ref[pl.ds(ks, DIL_T), :], preferred_element_type=F32)
            acc = pv if acc is None else acc + pv
        o_ref[rows, :] = _normalize(acc).astype(o_ref.dtype)


def _dil_attn(proj, slopes):
    s = proj.shape[0]
    nblk = s // DIL_T
    kern = functools.partial(_dil_attn_kernel, nblk=nblk)
    hd = DIL_HEAD_DIM
    return pl.pallas_call(
        kern,
        grid=(DIL_HEADS, nblk // DIL_SUB),
        in_specs=[
            pl.BlockSpec(memory_space=pltpu.SMEM),
            pl.BlockSpec((DIL_T * DIL_SUB, hd), lambda h, i: (i, COL_QB // hd + h)),
            pl.BlockSpec((s, hd), lambda h, i: (0, COL_KB // hd + h)),
            pl.BlockSpec((s, hd), lambda h, i: (0, COL_VB // hd + h)),
        ],
        out_specs=pl.BlockSpec((DIL_T * DIL_SUB, hd), lambda h, i: (i, h)),
        out_shape=jax.ShapeDtypeStruct((s, DIL_WIDTH), BF16),
        scratch_shapes=[
            pltpu.VMEM((DIL_NCHUNK + 1, DIL_T, DIL_T), F32),
            pltpu.VMEM((s, V_PAD), BF16),
        ],
        compiler_params=_params("arbitrary", "arbitrary"),
        name="dil_attn",
    )(slopes, proj, proj, proj)


def _gated_norm(a_ref, gate_ref, g_ref, y_ref):
    for r in range(0, a_ref.shape[0], ROW_CHUNK):
        rows = slice(r, r + ROW_CHUNK)
        gate = gate_ref[rows, :].astype(F32)
        y = _rms(a_ref[rows, :].astype(F32), g_ref[...]) * (gate * jax.nn.sigmoid(gate))
        y_ref[rows, :] = y.astype(y_ref.dtype)


def _tail_kernel(aa_ref, ab_ref, ga_ref, gb_ref, gma_ref, gmb_ref, x_ref, w_ref, gple_ref, p_ref, wp_ref,
                 gfin_ref, o_ref, yh_ref, *, nj, bn, last_layer):
    jj = pl.program_id(1)

    @pl.when(jj == 0)
    def _():
        _gated_norm(aa_ref, ga_ref, gma_ref, yh_ref.at[:, :MLA_WIDTH])
        _gated_norm(ab_ref, gb_ref, gmb_ref, yh_ref.at[:, MLA_WIDTH:])

    @pl.when(jj < nj)
    def _():
        cols = pl.ds(pl.multiple_of(jj * bn, bn), bn)
        o_ref[:, cols] = x_ref[...] + jnp.dot(yh_ref[...], w_ref[...], preferred_element_type=F32)

    @pl.when(jj == nj)
    def _():
        _rms_rows(o_ref, gple_ref, yh_ref)

    @pl.when(jj >= nj)
    def _():
        cols = pl.ds(pl.multiple_of((jj - nj) * bn, bn), bn)
        gate = jax.nn.sigmoid(jnp.dot(yh_ref[...], w_ref[...], preferred_element_type=F32))
        ple = jnp.dot(p_ref[...].astype(BF16), wp_ref[...], preferred_element_type=F32)
        o_ref[:, cols] = o_ref[:, cols] + ple * gate

    if last_layer:
        @pl.when(jj == 2 * nj - 1)
        def _():
            _rms_rows(o_ref, gfin_ref, o_ref)


def _tail(attn_a, attn_b, proj, g_a, g_b, x, w_cat, g_ple, p, w_ple, g_fin, last_layer, bm=512, bn=512):
    s, d = x.shape
    assert d == MLA_WIDTH + DIL_WIDTH
    c = p.shape[1]
    nj = d // bn
    kern = functools.partial(_tail_kernel, nj=nj, bn=bn, last_layer=last_layer)
    first = lambda i, jj: (i, jnp.minimum(jj, nj - 1))
    second = lambda i, jj: (0, jnp.maximum(jj - nj, 0))
    row = lambda i, jj: (i, 0)
    vec = lambda i, jj: (0, 0)
    return pl.pallas_call(
        kern,
        grid=(s // bm, 2 * nj),
        in_specs=[
            pl.BlockSpec((bm, MLA_WIDTH), row),
            pl.BlockSpec((bm, DIL_WIDTH), row),
            pl.BlockSpec((bm, MLA_WIDTH), lambda i, jj: (i, COL_GATE_A // MLA_WIDTH)),
            pl.BlockSpec((bm, DIL_WIDTH), lambda i, jj: (i, COL_GATE_B // DIL_WIDTH)),
            pl.BlockSpec((1, MLA_WIDTH), vec),
            pl.BlockSpec((1, DIL_WIDTH), vec),
            pl.BlockSpec((bm, bn), first),
            pl.BlockSpec((d, bn), lambda i, jj: (jj // nj, jj % nj)),
            pl.BlockSpec((1, d), vec),
            pl.BlockSpec((bm, c), row),
            pl.BlockSpec((c, bn), second),
            pl.BlockSpec((1, d), vec),
        ],
        out_specs=pl.BlockSpec((bm, d), row),
        out_shape=jax.ShapeDtypeStruct((s, d), F32),
        scratch_shapes=[pltpu.VMEM((bm, d), BF16)],
        compiler_params=_params("parallel", "arbitrary"),
        name="tail",
    )(attn_a, attn_b, proj, proj, g_a, g_b, x, w_cat, g_ple, p, w_ple, g_fin)


def _rope_dup(w, axis=-1):
    t1, t2 = jnp.split(w, 2, axis=axis)
    return jnp.concatenate([t1, t2, t2, t1], axis=axis)


def _prep_w_in(w):
    wt = w.T
    d = wt.shape[1]
    lat = Q_LORA + KV_LORA
    head = lat + QK_ROPE
    row_scale = np.ones((wt.shape[0] - head, 1), np.float32)
    row_scale[COL_QB:COL_KB] = DIL_HEAD_DIM ** -0.5 * LOG2E
    w_main = (wt[head:] * row_scale).astype(BF16)
    small = lax.optimization_barrier(wt[:head])
    pad = jnp.zeros((PROJ_WIDTH - COL_KR - LANES, d), wt.dtype)
    w_tail = jnp.concatenate([small[:lat], _rope_dup(small[lat:], axis=0), pad], axis=0).astype(BF16)
    return w_main, w_tail


def _prep_w_uq(w):
    r = w.shape[0]
    w = w.reshape(r, MLA_HEADS, QK_NOPE + QK_ROPE) * ((QK_NOPE + QK_ROPE) ** -0.5 * LOG2E)
    w = jnp.concatenate([w[..., :QK_NOPE], _rope_dup(w[..., QK_NOPE:])], axis=-1)
    return w.reshape(r, MLA_HEADS * QK_PAD).astype(BF16)


def _rope_tables(positions):
    inv = ROPE_THETA ** (-jnp.arange(ROPE_HALF, dtype=F32) / ROPE_HALF)
    ang = positions.astype(F32)[:, None] * inv
    cos, sin = jnp.cos(ang), jnp.sin(ang)
    zero = jnp.zeros_like(cos)
    return (jnp.concatenate([cos, cos, zero, zero], axis=1),
            jnp.concatenate([-sin, sin, zero, zero], axis=1))


def kernel(x, p, positions, g_mix, w_in, g_q_latent, w_uq, g_kv_latent, w_ukv, g_out_mla, g_out_dil,
           w_out, w_ple, g_ple, w_ple_gate, g_final):
    b, s, d = x.shape
    assert b == 1 and s % (DIL_T * DIL_SUB) == 0
    slopes = jnp.asarray(2.0 ** (-ALIBI_MAX_BIAS * np.arange(1, DIL_HEADS + 1) / DIL_HEADS), F32)
    x2d = x.reshape(s, d)
    cos, sin = _rope_tables(positions[0])
    depth = g_mix.shape[0]
    for i in range(depth):
        proj = _in_proj(x2d, g_mix[i][None], *_prep_w_in(w_in[i]))
        q_a, k_a, v_a = _mla_prep(proj, g_q_latent[i][None], g_kv_latent[i][None], cos, sin,
                                  _prep_w_uq(w_uq[i]), w_ukv[i].astype(BF16))
        attn_a = _mla_attn(q_a, k_a, v_a)
        attn_b = _dil_attn(proj, slopes)
        w_cat = lax.dynamic_update_slice(jnp.zeros((2 * d, d), BF16), w_out[i].astype(BF16), (0, 0))
        w_cat = lax.dynamic_update_slice(w_cat, w_ple_gate[i].astype(BF16), (d, 0))
        x2d = _tail(attn_a, attn_b, proj, g_out_mla[i][None], g_out_dil[i][None], x2d, w_cat,
                    g_ple[i][None], p[i, 0], w_ple[i].astype(BF16), g_final[None], i == depth - 1)
    return x2d.reshape(b, s, d)
```

```python
import functools

import numpy as np
import jax
import jax.numpy as jnp
from jax import lax
from jax.experimental import pallas as pl
from jax.experimental.pallas import tpu as pltpu

F32 = jnp.float32
BF16 = jnp.bfloat16

EPS = 1e-6
ROPE_THETA = 10000.0
ALIBI_MAX_BIAS = 8.0

MLA_HEADS = 16
Q_LORA = 1024
KV_LORA = 512
QK_NOPE = 128
QK_ROPE = 64
V_HEAD = 128
MLA_WIDTH = MLA_HEADS * V_HEAD
DIL_HEADS = 16
DIL_HEAD_DIM = 128
DIL_WIDTH = DIL_HEADS * DIL_HEAD_DIM
DIL_PATTERNS = ((128, 1), (512, 4), (2048, 16))

LANES = 128
QK_PAD = 256
V_PAD = 2 * LANES
ROPE_HALF = QK_ROPE // 2
LOG2E = float(np.log2(np.e))

COL_GATE_A = 0
COL_QB = COL_GATE_A + MLA_WIDTH
COL_KB = COL_QB + DIL_WIDTH
COL_VB = COL_KB + DIL_WIDTH
COL_GATE_B = COL_VB + DIL_WIDTH
COL_CQ = COL_GATE_B + DIL_WIDTH
COL_CKV = COL_CQ + Q_LORA
COL_KR = COL_CKV + KV_LORA
PROJ_WIDTH = 12288

MASK_BIAS = -1e30

ROW_CHUNK = 64

VMEM_LIMIT = 60 * 1024 * 1024


def _params(*sem):
    return pltpu.CompilerParams(dimension_semantics=sem, vmem_limit_bytes=VMEM_LIMIT)


def _rms(x, g):
    return x * lax.rsqrt(jnp.mean(x * x, axis=-1, keepdims=True) + EPS) * g


def _rms_rows(x_ref, g_ref, h_ref):
    for r in range(0, x_ref.shape[0], ROW_CHUNK):
        rows = slice(r, r + ROW_CHUNK)
        h_ref[rows, :] = _rms(x_ref[rows, :].astype(F32), g_ref[...]).astype(h_ref.dtype)


_NT = (((1,), (1,)), ((), ()))


def _in_proj_kernel(x_ref, g_ref, wm_ref, wt_ref, o_ref, h_ref, *, n_main):
    j = pl.program_id(1)

    @pl.when(j == 0)
    def _():
        _rms_rows(x_ref, g_ref, h_ref)

    @pl.when(j < n_main)
    def _():
        o_ref[...] = lax.dot_general(h_ref[...], wm_ref[...], _NT, preferred_element_type=F32).astype(o_ref.dtype)

    @pl.when(j >= n_main)
    def _():
        o_ref[...] = lax.dot_general(h_ref[...], wt_ref[...], _NT, preferred_element_type=F32).astype(o_ref.dtype)


def _in_proj(x, g, w_main, w_tail, bm=512, bn=1024):
    s, d = x.shape
    n_main = w_main.shape[0] // bn
    n_tail = w_tail.shape[0] // bn
    kern = functools.partial(_in_proj_kernel, n_main=n_main)
    return pl.pallas_call(
        kern,
        grid=(s // bm, n_main + n_tail),
        in_specs=[
            pl.BlockSpec((bm, d), lambda i, j: (i, 0)),
            pl.BlockSpec((1, d), lambda i, j: (0, 0)),
            pl.BlockSpec((bn, d), lambda i, j: (jnp.minimum(j, n_main - 1), 0)),
            pl.BlockSpec((bn, d), lambda i, j: (jnp.maximum(j - n_main, 0), 0)),
        ],
        out_specs=pl.BlockSpec((bm, bn), lambda i, j: (i, j)),
        out_shape=jax.ShapeDtypeStruct((s, (n_main + n_tail) * bn), BF16),
        scratch_shapes=[pltpu.VMEM((bm, d), BF16)],
        compiler_params=_params("parallel", "arbitrary"),
        name="in_proj",
    )(x, g, w_main, w_tail)


def _rope(t, cos, sin):
    lane = lax.broadcasted_iota(jnp.int32, t.shape, 1)
    swapped = jnp.where(lane < ROPE_HALF, pltpu.roll(t, LANES - ROPE_HALF, axis=1), pltpu.roll(t, ROPE_HALF, axis=1))
    return t * cos + swapped * sin


def _mla_prep_kernel(cq_ref, ckv_ref, kr_ref, gq_ref, gkv_ref, cos_ref, sin_ref, wq_ref, wkv_ref,
                     q_out, k_out, v_out, cqn_ref, ckvn_ref, kpe_ref, *, heads):
    @pl.when(pl.program_id(1) == 0)
    def _():
        cqn_ref[...] = _rms(cq_ref[...].astype(F32), gq_ref[...]).astype(BF16)
        ckvn_ref[...] = _rms(ckv_ref[...].astype(F32), gkv_ref[...]).astype(BF16)
        kpe_ref[...] = _rope(kr_ref[...].astype(F32), cos_ref[...], sin_ref[...]).astype(BF16)

    q = jnp.dot(cqn_ref[...], wq_ref[...], preferred_element_type=F32)
    kv = jnp.dot(ckvn_ref[...], wkv_ref[...], preferred_element_type=F32)
    cos = cos_ref[...]
    sin = sin_ref[...]
    for g in range(heads):
        lo = g * QK_PAD
        mid = lo + LANES
        hi = lo + QK_PAD
        q_out[:, lo:mid] = q[:, lo:mid].astype(BF16)
        q_out[:, mid:hi] = _rope(q[:, mid:hi], cos, sin).astype(BF16)
        k_out[:, lo:mid] = kv[:, lo:mid].astype(BF16)
        k_out[:, mid:hi] = kpe_ref[...]
        v_out[:, lo:mid] = kv[:, mid:hi].astype(BF16)
        v_out[:, mid:hi] = jnp.ones((v_out.shape[0], LANES), BF16)


def _mla_prep(proj, gq, gkv, cos, sin, wq, wkv, bm=512, heads=8):
    s = proj.shape[0]
    bq = heads * QK_PAD
    kern = functools.partial(_mla_prep_kernel, heads=heads)
    return pl.pallas_call(
        kern,
        grid=(s // bm, MLA_HEADS // heads),
        in_specs=[
            pl.BlockSpec((bm, Q_LORA), lambda i, j: (i, COL_CQ // Q_LORA)),
            pl.BlockSpec((bm, KV_LORA), lambda i, j: (i, COL_CKV // KV_LORA)),
            pl.BlockSpec((bm, LANES), lambda i, j: (i, COL_KR // LANES)),
            pl.BlockSpec((1, Q_LORA), lambda i, j: (0, 0)),
            pl.BlockSpec((1, KV_LORA), lambda i, j: (0, 0)),
            pl.BlockSpec((bm, LANES), lambda i, j: (i, 0)),
            pl.BlockSpec((bm, LANES), lambda i, j: (i, 0)),
            pl.BlockSpec((Q_LORA, bq), lambda i, j: (0, j)),
            pl.BlockSpec((KV_LORA, bq), lambda i, j: (0, j)),
        ],
        out_specs=[
            pl.BlockSpec((bm, bq), lambda i, j: (i, j)),
            pl.BlockSpec((bm, bq), lambda i, j: (i, j)),
            pl.BlockSpec((bm, heads * V_PAD), lambda i, j: (i, j)),
        ],
        out_shape=[
            jax.ShapeDtypeStruct((s, MLA_HEADS * QK_PAD), BF16),
            jax.ShapeDtypeStruct((s, MLA_HEADS * QK_PAD), BF16),
            jax.ShapeDtypeStruct((s, MLA_HEADS * V_PAD), BF16),
        ],
        scratch_shapes=[
            pltpu.VMEM((bm, Q_LORA), BF16),
            pltpu.VMEM((bm, KV_LORA), BF16),
            pltpu.VMEM((bm, LANES), BF16),
        ],
        compiler_params=_params("parallel", "arbitrary"),
        name="mla_prep",
    )(proj, proj, proj, gq, gkv, cos, sin, wq, wkv)


def _normalize(acc):
    return acc[:, :V_HEAD] / acc[:, V_HEAD:V_HEAD + 1]


def _mla_attn_kernel(q_ref, k_ref, v_ref, o_ref, *, tq, tk):
    for u in range(q_ref.shape[0] // tq):
        q_rows = slice(u * tq, (u + 1) * tq)
        q = q_ref[q_rows, :]
        m = acc = None
        for c in range(k_ref.shape[0] // tk):
            rows = slice(c * tk, (c + 1) * tk)
            s = lax.dot_general(q, k_ref[rows, :], _NT, preferred_element_type=F32)
            m_c = jnp.max(s, axis=-1, keepdims=True)
            if c == 0:
                m = m_c
                acc = jnp.dot(jnp.exp2(s - m).astype(BF16), v_ref[rows, :], preferred_element_type=F32)
            else:
                m_new = jnp.maximum(m, m_c)
                pv = jnp.dot(jnp.exp2(s - m_new).astype(BF16), v_ref[rows, :], preferred_element_type=F32)
                acc = jnp.exp2(m - m_new) * acc + pv
                m = m_new
        o_ref[q_rows, :] = _normalize(acc).astype(o_ref.dtype)


def _mla_attn(q, k, v, tq=512, sub=2, tk=1024):
    s = q.shape[0]
    kern = functools.partial(_mla_attn_kernel, tq=tq, tk=tk)
    return pl.pallas_call(
        kern,
        grid=(MLA_HEADS, s // (tq * sub)),
        in_specs=[
            pl.BlockSpec((tq * sub, QK_PAD), lambda h, i: (i, h)),
            pl.BlockSpec((s, QK_PAD), lambda h, i: (0, h)),
            pl.BlockSpec((s, V_PAD), lambda h, i: (0, h)),
        ],
        out_specs=pl.BlockSpec((tq * sub, V_HEAD), lambda h, i: (i, h)),
        out_shape=jax.ShapeDtypeStruct((s, MLA_WIDTH), BF16),
        compiler_params=_params("parallel", "arbitrary"),
        name="mla_attn",
    )(q, k, v)


DIL_T = 256
DIL_SUB = 8
DIL_REACH = max(w // 2 for w, _ in DIL_PATTERNS)
DIL_SIDE = DIL_REACH // DIL_T
DIL_NCHUNK = 2 * DIL_SIDE + 1


def _dil_bias(r, slope):
    row = lax.broadcasted_iota(jnp.int32, (DIL_T, DIL_T), 0)
    col = lax.broadcasted_iota(jnp.int32, (DIL_T, DIL_T), 1)
    d = (r - DIL_SIDE) * DIL_T + col - row
    ad = jnp.abs(d)
    cnt = jnp.zeros((DIL_T, DIL_T), F32)
    for window, dil in DIL_PATTERNS:
        member = ((d & (dil - 1)) == 0) & (ad <= window // 2)
        cnt = cnt + member.astype(F32)
    bias = jnp.log2(jnp.maximum(cnt, 1.0)) - (slope * LOG2E) * ad.astype(F32)
    return jnp.where(cnt > 0.0, bias, MASK_BIAS)


def _dil_attn_kernel(slope_ref, q_ref, k_ref, v_ref, o_ref, b_ref, vp_ref, *, nblk):
    h = pl.program_id(0)
    i = pl.program_id(1)

    @pl.when(i == 0)
    def _():
        slope = slope_ref[h]
        for r in range(DIL_NCHUNK):
            b_ref[r] = _dil_bias(r, slope)
        b_ref[DIL_NCHUNK] = jnp.full((DIL_T, DIL_T), MASK_BIAS, F32)
        vp_ref[:, :DIL_HEAD_DIM] = v_ref[...]
        vp_ref[:, DIL_HEAD_DIM:] = jnp.ones((vp_ref.shape[0], V_PAD - DIL_HEAD_DIM), BF16)

    for u in range(DIL_SUB):
        rows = slice(u * DIL_T, (u + 1) * DIL_T)
        q = q_ref[rows, :]
        scores, starts = [], []
        for r in range(DIL_NCHUNK):
            kc = i * DIL_SUB + u + (r - DIL_SIDE)
            inside = (kc >= 0) & (kc < nblk)
            ks = pl.multiple_of(jnp.clip(kc, 0, nblk - 1) * DIL_T, DIL_T)
            bias = b_ref[jnp.where(inside, r, DIL_NCHUNK)]
            scores.append(lax.dot_general(q, k_ref[pl.ds(ks, DIL_T), :], _NT, preferred_element_type=F32) + bias)
            starts.append(ks)
        m = functools.reduce(jnp.maximum, [jnp.max(s, axis=-1, keepdims=True) for s in scores])
        acc = None
        for s, ks in zip(scores, starts):
            pv = jnp.dot(jnp.exp2(s - m).astype(BF16), vp_ref[pl.ds(ks, DIL_T), :], preferred_element_type=F32)
            acc = pv if acc is None else acc + pv
        o_ref[rows, :] = _normalize(acc).astype(o_ref.dtype)


def _dil_attn(proj, slopes):
    s = proj.shape[0]
    nblk = s // DIL_T
    kern = functools.partial(_dil_attn_kernel, nblk=nblk)
    hd = DIL_HEAD_DIM
    return pl.pallas_call(
        kern,
        grid=(DIL_HEADS, nblk // DIL_SUB),
        in_specs=[
            pl.BlockSpec(memory_space=pltpu.SMEM),
            pl.BlockSpec((DIL_T * DIL_SUB, hd), lambda h, i: (i, COL_QB // hd + h)),
            pl.BlockSpec((s, hd), lambda h, i: (0, COL_KB // hd + h)),
            pl.BlockSpec((s, hd), lambda h, i: (0, COL_VB // hd + h)),
        ],
        out_specs=pl.BlockSpec((DIL_T * DIL_SUB, hd), lambda h, i: (i, h)),
        out_shape=jax.ShapeDtypeStruct((s, DIL_WIDTH), BF16),
        scratch_shapes=[
            pltpu.VMEM((DIL_NCHUNK + 1, DIL_T, DIL_T), F32),
            pltpu.VMEM((s, V_PAD), BF16),
        ],
        compiler_params=_params("arbitrary", "arbitrary"),
        name="dil_attn",
    )(slopes, proj, proj, proj)


def _gated_norm(a_ref, gate_ref, g_ref, y_ref):
    for r in range(0, a_ref.shape[0], ROW_CHUNK):
        rows = slice(r, r + ROW_CHUNK)
        gate = gate_ref[rows, :].astype(F32)
        y = _rms(a_ref[rows, :].astype(F32), g_ref[...]) * (gate * jax.nn.sigmoid(gate))
        y_ref[rows, :] = y.astype(y_ref.dtype)


def _tail_kernel(aa_ref, ab_ref, ga_ref, gb_ref, gma_ref, gmb_ref, x_ref, w_ref, gple_ref, p_ref, wp_ref,
                 gfin_ref, o_ref, yh_ref, *, nj, bn, last_layer):
    jj = pl.program_id(1)

    @pl.when(jj == 0)
    def _():
        _gated_norm(aa_ref, ga_ref, gma_ref, yh_ref.at[:, :MLA_WIDTH])
        _gated_norm(ab_ref, gb_ref, gmb_ref, yh_ref.at[:, MLA_WIDTH:])

    @pl.when(jj < nj)
    def _():
        cols = pl.ds(pl.multiple_of(jj * bn, bn), bn)
        o_ref[:, cols] = x_ref[...] + jnp.dot(yh_ref[...], w_ref[...], preferred_element_type=F32)

    @pl.when(jj == nj)
    def _():
        _rms_rows(o_ref, gple_ref, yh_ref)

    @pl.when(jj >= nj)
    def _():
        cols = pl.ds(pl.multiple_of((jj - nj) * bn, bn), bn)
        gate = jax.nn.sigmoid(jnp.dot(yh_ref[...], w_ref[...], preferred_element_type=F32))
        ple = jnp.dot(p_ref[...].astype(BF16), wp_ref[...], preferred_element_type=F32)
        o_ref[:, cols] = o_ref[:, cols] + ple * gate

    if last_layer:
        @pl.when(jj == 2 * nj - 1)
        def _():
            _rms_rows(o_ref, gfin_ref, o_ref)


def _tail(attn_a, attn_b, proj, g_a, g_b, x, w_cat, g_ple, p, w_ple, g_fin, last_layer, bm=512, bn=512):
    s, d = x.shape
    assert d == MLA_WIDTH + DIL_WIDTH
    c = p.shape[1]
    nj = d // bn
    kern = functools.partial(_tail_kernel, nj=nj, bn=bn, last_layer=last_layer)
    first = lambda i, jj: (i, jnp.minimum(jj, nj - 1))
    second = lambda i, jj: (0, jnp.maximum(jj - nj, 0))
    row = lambda i, jj: (i, 0)
    vec = lambda i, jj: (0, 0)
    return pl.pallas_call(
        kern,
        grid=(s // bm, 2 * nj),
        in_specs=[
            pl.BlockSpec((bm, MLA_WIDTH), row),
            pl.BlockSpec((bm, DIL_WIDTH), row),
            pl.BlockSpec((bm, MLA_WIDTH), lambda i, jj: (i, COL_GATE_A // MLA_WIDTH)),
            pl.BlockSpec((bm, DIL_WIDTH), lambda i, jj: (i, COL_GATE_B // DIL_WIDTH)),
            pl.BlockSpec((1, MLA_WIDTH), vec),
            pl.BlockSpec((1, DIL_WIDTH), vec),
            pl.BlockSpec((bm, bn), first),
            pl.BlockSpec((d, bn), lambda i, jj: (jj // nj, jj % nj)),
            pl.BlockSpec((1, d), vec),
            pl.BlockSpec((bm, c), row),
            pl.BlockSpec((c, bn), second),
            pl.BlockSpec((1, d), vec),
        ],
        out_specs=pl.BlockSpec((bm, d), row),
        out_shape=jax.ShapeDtypeStruct((s, d), F32),
        scratch_shapes=[pltpu.VMEM((bm, d), BF16)],
        compiler_params=_params("parallel", "arbitrary"),
        name="tail",
    )(attn_a, attn_b, proj, proj, g_a, g_b, x, w_cat, g_ple, p, w_ple, g_fin)


def _stack_cast_kernel(a_ref, b_ref, o_ref, *, nb):
    t = pl.program_id(0)

    @pl.when(t < nb)
    def _():
        o_ref[...] = a_ref[...].astype(o_ref.dtype)

    @pl.when(t >= nb)
    def _():
        o_ref[...] = b_ref[...].astype(o_ref.dtype)


def _stack_cast(a, b, rb=512):
    r, c = a.shape
    assert a.shape == b.shape and r % rb == 0
    nb = r // rb
    return pl.pallas_call(
        functools.partial(_stack_cast_kernel, nb=nb),
        grid=(2 * nb,),
        in_specs=[
            pl.BlockSpec((rb, c), lambda t: (jnp.minimum(t, nb - 1), 0)),
            pl.BlockSpec((rb, c), lambda t: (jnp.maximum(t - nb, 0), 0)),
        ],
        out_specs=pl.BlockSpec((rb, c), lambda t: (t, 0)),
        out_shape=jax.ShapeDtypeStruct((2 * r, c), BF16),
        compiler_params=_params("arbitrary"),
        name="stack_cast",
    )(a, b)


def _prep_w_in(w):
    wt = w.T
    d = wt.shape[1]
    lat = Q_LORA + KV_LORA
    head = lat + QK_ROPE
    row_scale = np.ones((wt.shape[0] - head, 1), np.float32)
    row_scale[COL_QB:COL_KB] = DIL_HEAD_DIM ** -0.5 * LOG2E
    w_main = (wt[head:] * row_scale).astype(BF16)
    small = lax.optimization_barrier(wt[:head])
    w_tail = jnp.pad(small, ((0, PROJ_WIDTH - COL_KR - QK_ROPE), (0, 0))).astype(BF16)
    return w_main, w_tail


def _prep_w_uq(w):
    r = w.shape[0]
    w = w.reshape(r, MLA_HEADS, QK_NOPE + QK_ROPE) * ((QK_NOPE + QK_ROPE) ** -0.5 * LOG2E)
    w = jnp.pad(w, ((0, 0), (0, 0), (0, QK_PAD - QK_NOPE - QK_ROPE)))
    return w.reshape(r, MLA_HEADS * QK_PAD).astype(BF16)


def _rope_tables(positions):
    inv = ROPE_THETA ** (-jnp.arange(ROPE_HALF, dtype=F32) / ROPE_HALF)
    ang = positions.astype(F32)[:, None] * inv
    cos, sin = jnp.cos(ang), jnp.sin(ang)
    zero = jnp.zeros_like(cos)
    return (jnp.concatenate([cos, cos, zero, zero], axis=1),
            jnp.concatenate([-sin, sin, zero, zero], axis=1))


def kernel(x, p, positions, g_mix, w_in, g_q_latent, w_uq, g_kv_latent, w_ukv, g_out_mla, g_out_dil,
           w_out, w_ple, g_ple, w_ple_gate, g_final):
    b, s, d = x.shape
    assert b == 1 and s % (DIL_T * DIL_SUB) == 0
    slopes = jnp.asarray(2.0 ** (-ALIBI_MAX_BIAS * np.arange(1, DIL_HEADS + 1) / DIL_HEADS), F32)
    x2d = x.reshape(s, d)
    cos, sin = _rope_tables(positions[0])
    depth = g_mix.shape[0]
    for i in range(depth):
        proj = _in_proj(x2d, g_mix[i][None], *_prep_w_in(w_in[i]))
        q_a, k_a, v_a = _mla_prep(proj, g_q_latent[i][None], g_kv_latent[i][None], cos, sin,
                                  _prep_w_uq(w_uq[i]), w_ukv[i].astype(BF16))
        attn_a = _mla_attn(q_a, k_a, v_a)
        attn_b = _dil_attn(proj, slopes)
        w_cat = _stack_cast(w_out[i], w_ple_gate[i])
        x2d = _tail(attn_a, attn_b, proj, g_out_mla[i][None], g_out_dil[i][None], x2d, w_cat,
                    g_ple[i][None], p[i, 0], w_ple[i].astype(BF16), g_final[None], i == depth - 1)
    return x2d.reshape(b, s, d)
```

```python
import functools

import numpy as np
import jax
import jax.numpy as jnp
from jax import lax
from jax.experimental import pallas as pl
from jax.experimental.pallas import tpu as pltpu

F32 = jnp.float32
BF16 = jnp.bfloat16

EPS = 1e-6
ROPE_THETA = 10000.0
ALIBI_MAX_BIAS = 8.0

MLA_HEADS = 16
Q_LORA = 1024
KV_LORA = 512
QK_NOPE = 128
QK_ROPE = 64
V_HEAD = 128
MLA_WIDTH = MLA_HEADS * V_HEAD
DIL_HEADS = 16
DIL_HEAD_DIM = 128
DIL_WIDTH = DIL_HEADS * DIL_HEAD_DIM
DIL_PATTERNS = ((128, 1), (512, 4), (2048, 16))

LANES = 128
QK_PAD = 256
V_PAD = 2 * LANES
ROPE_HALF = QK_ROPE // 2
LOG2E = float(np.log2(np.e))

COL_GATE_A = 0
COL_QB = COL_GATE_A + MLA_WIDTH
COL_KB = COL_QB + DIL_WIDTH
COL_VB = COL_KB + DIL_WIDTH
COL_GATE_B = COL_VB + DIL_WIDTH
COL_CQ = COL_GATE_B + DIL_WIDTH
COL_CKV = COL_CQ + Q_LORA
COL_KR = COL_CKV + KV_LORA
PROJ_WIDTH = 12288

MASK_BIAS = -1e30

ROW_CHUNK = 64

VMEM_LIMIT = 60 * 1024 * 1024


def _params(*sem):
    return pltpu.CompilerParams(dimension_semantics=sem, vmem_limit_bytes=VMEM_LIMIT)


def _rms(x, g):
    return x * lax.rsqrt(jnp.mean(x * x, axis=-1, keepdims=True) + EPS) * g


def _rms_rows(x_ref, g_ref, h_ref):
    for r in range(0, x_ref.shape[0], ROW_CHUNK):
        rows = slice(r, r + ROW_CHUNK)
        h_ref[rows, :] = _rms(x_ref[rows, :].astype(F32), g_ref[...]).astype(h_ref.dtype)


_NT = (((1,), (1,)), ((), ()))


def _in_proj_kernel(x_ref, g_ref, wm_ref, wt_ref, o_ref, h_ref, *, n_main):
    j = pl.program_id(1)

    @pl.when(j == 0)
    def _():
        _rms_rows(x_ref, g_ref, h_ref)

    @pl.when(j < n_main)
    def _():
        o_ref[...] = lax.dot_general(h_ref[...], wm_ref[...], _NT, preferred_element_type=F32).astype(o_ref.dtype)

    @pl.when(j >= n_main)
    def _():
        o_ref[...] = lax.dot_general(h_ref[...], wt_ref[...], _NT, preferred_element_type=F32).astype(o_ref.dtype)


def _in_proj(x, g, w_main, w_tail, bm=512, bn=1024):
    s, d = x.shape
    n_main = w_main.shape[0] // bn
    n_tail = w_tail.shape[0] // bn
    kern = functools.partial(_in_proj_kernel, n_main=n_main)
    return pl.pallas_call(
        kern,
        grid=(s // bm, n_main + n_tail),
        in_specs=[
            pl.BlockSpec((bm, d), lambda i, j: (i, 0)),
            pl.BlockSpec((1, d), lambda i, j: (0, 0)),
            pl.BlockSpec((bn, d), lambda i, j: (jnp.minimum(j, n_main - 1), 0)),
            pl.BlockSpec((bn, d), lambda i, j: (jnp.maximum(j - n_main, 0), 0)),
        ],
        out_specs=pl.BlockSpec((bm, bn), lambda i, j: (i, j)),
        out_shape=jax.ShapeDtypeStruct((s, (n_main + n_tail) * bn), BF16),
        scratch_shapes=[pltpu.VMEM((bm, d), BF16)],
        compiler_params=_params("parallel", "arbitrary"),
        name="in_proj",
    )(x, g, w_main, w_tail)


def _rope(t, cos, sin):
    lane = lax.broadcasted_iota(jnp.int32, t.shape, 1)
    swapped = jnp.where(lane < ROPE_HALF, pltpu.roll(t, LANES - ROPE_HALF, axis=1), pltpu.roll(t, ROPE_HALF, axis=1))
    return t * cos + swapped * sin


def _mla_prep_kernel(cq_ref, ckv_ref, kr_ref, gq_ref, gkv_ref, cos_ref, sin_ref, wq_ref, wkv_ref,
                     q_out, k_out, v_out, cqn_ref, ckvn_ref, kpe_ref, *, heads):
    @pl.when(pl.program_id(1) == 0)
    def _():
        cqn_ref[...] = _rms(cq_ref[...].astype(F32), gq_ref[...]).astype(BF16)
        ckvn_ref[...] = _rms(ckv_ref[...].astype(F32), gkv_ref[...]).astype(BF16)
        kpe_ref[...] = _rope(kr_ref[...].astype(F32), cos_ref[...], sin_ref[...]).astype(BF16)

    q = jnp.dot(cqn_ref[...], wq_ref[...], preferred_element_type=F32)
    kv = jnp.dot(ckvn_ref[...], wkv_ref[...], preferred_element_type=F32)
    cos = cos_ref[...]
    sin = sin_ref[...]
    for g in range(heads):
        lo = g * QK_PAD
        mid = lo + LANES
        hi = lo + QK_PAD
        q_out[:, lo:mid] = q[:, lo:mid].astype(BF16)
        q_out[:, mid:hi] = _rope(q[:, mid:hi], cos, sin).astype(BF16)
        k_out[:, lo:mid] = kv[:, lo:mid].astype(BF16)
        k_out[:, mid:hi] = kpe_ref[...]
        v_out[:, lo:mid] = kv[:, mid:hi].astype(BF16)
        v_out[:, mid:hi] = jnp.ones((v_out.shape[0], LANES), BF16)


def _mla_prep(proj, gq, gkv, cos, sin, wq, wkv, bm=256, heads=MLA_HEADS):
    s = proj.shape[0]
    bq = heads * QK_PAD
    kern = functools.partial(_mla_prep_kernel, heads=heads)
    return pl.pallas_call(
        kern,
        grid=(s // bm, MLA_HEADS // heads),
        in_specs=[
            pl.BlockSpec((bm, Q_LORA), lambda i, j: (i, COL_CQ // Q_LORA)),
            pl.BlockSpec((bm, KV_LORA), lambda i, j: (i, COL_CKV // KV_LORA)),
            pl.BlockSpec((bm, LANES), lambda i, j: (i, COL_KR // LANES)),
            pl.BlockSpec((1, Q_LORA), lambda i, j: (0, 0)),
            pl.BlockSpec((1, KV_LORA), lambda i, j: (0, 0)),
            pl.BlockSpec((bm, LANES), lambda i, j: (i, 0)),
            pl.BlockSpec((bm, LANES), lambda i, j: (i, 0)),
            pl.BlockSpec((Q_LORA, bq), lambda i, j: (0, j)),
            pl.BlockSpec((KV_LORA, bq), lambda i, j: (0, j)),
        ],
        out_specs=[
            pl.BlockSpec((bm, bq), lambda i, j: (i, j)),
            pl.BlockSpec((bm, bq), lambda i, j: (i, j)),
            pl.BlockSpec((bm, heads * V_PAD), lambda i, j: (i, j)),
        ],
        out_shape=[
            jax.ShapeDtypeStruct((s, MLA_HEADS * QK_PAD), BF16),
            jax.ShapeDtypeStruct((s, MLA_HEADS * QK_PAD), BF16),
            jax.ShapeDtypeStruct((s, MLA_HEADS * V_PAD), BF16),
        ],
        scratch_shapes=[
            pltpu.VMEM((bm, Q_LORA), BF16),
            pltpu.VMEM((bm, KV_LORA), BF16),
            pltpu.VMEM((bm, LANES), BF16),
        ],
        compiler_params=_params("parallel", "arbitrary"),
        name="mla_prep",
    )(proj, proj, proj, gq, gkv, cos, sin, wq, wkv)


def _normalize(acc):
    return acc[:, :V_HEAD] / acc[:, V_HEAD:V_HEAD + 1]


def _mla_attn_kernel(q_ref, k_ref, v_ref, o_ref, *, tq, tk):
    for u in range(q_ref.shape[0] // tq):
        q_rows = slice(u * tq, (u + 1) * tq)
        q = q_ref[q_rows, :]
        m = acc = None
        for c in range(k_ref.shape[0] // tk):
            rows = slice(c * tk, (c + 1) * tk)
            s = lax.dot_general(q, k_ref[rows, :], _NT, preferred_element_type=F32)
            m_c = jnp.max(s, axis=-1, keepdims=True)
            if c == 0:
                m = m_c
                acc = jnp.dot(jnp.exp2(s - m).astype(BF16), v_ref[rows, :], preferred_element_type=F32)
            else:
                m_new = jnp.maximum(m, m_c)
                pv = jnp.dot(jnp.exp2(s - m_new).astype(BF16), v_ref[rows, :], preferred_element_type=F32)
                acc = jnp.exp2(m - m_new) * acc + pv
                m = m_new
        o_ref[q_rows, :] = _normalize(acc).astype(o_ref.dtype)


def _mla_attn(q, k, v, tq=512, sub=4, tk=1024):
    s = q.shape[0]
    kern = functools.partial(_mla_attn_kernel, tq=tq, tk=tk)
    return pl.pallas_call(
        kern,
        grid=(MLA_HEADS, s // (tq * sub)),
        in_specs=[
            pl.BlockSpec((tq * sub, QK_PAD), lambda h, i: (i, h)),
            pl.BlockSpec((s, QK_PAD), lambda h, i: (0, h)),
            pl.BlockSpec((s, V_PAD), lambda h, i: (0, h)),
        ],
        out_specs=pl.BlockSpec((tq * sub, V_HEAD), lambda h, i: (i, h)),
        out_shape=jax.ShapeDtypeStruct((s, MLA_WIDTH), BF16),
        compiler_params=_params("parallel", "arbitrary"),
        name="mla_attn",
    )(q, k, v)


DIL_T = 256
DIL_SUB = 8
DIL_REACH = max(w // 2 for w, _ in DIL_PATTERNS)
DIL_SIDE = DIL_REACH // DIL_T
DIL_NCHUNK = 2 * DIL_SIDE + 1


def _dil_bias(r, slope):
    row = lax.broadcasted_iota(jnp.int32, (DIL_T, DIL_T), 0)
    col = lax.broadcasted_iota(jnp.int32, (DIL_T, DIL_T), 1)
    d = (r - DIL_SIDE) * DIL_T + col - row
    ad = jnp.abs(d)
    cnt = jnp.zeros((DIL_T, DIL_T), F32)
    for window, dil in DIL_PATTERNS:
        member = ((d & (dil - 1)) == 0) & (ad <= window // 2)
        cnt = cnt + member.astype(F32)
    bias = jnp.log2(jnp.maximum(cnt, 1.0)) - (slope * LOG2E) * ad.astype(F32)
    return jnp.where(cnt > 0.0, bias, MASK_BIAS)


def _dil_attn_kernel(slope_ref, q_ref, k_ref, v_ref, o_ref, b_ref, vp_ref, *, nblk):
    h = pl.program_id(0)
    i = pl.program_id(1)

    @pl.when(i == 0)
    def _():
        slope = slope_ref[h]
        for r in range(DIL_NCHUNK):
            b_ref[r] = _dil_bias(r, slope)
        b_ref[DIL_NCHUNK] = jnp.full((DIL_T, DIL_T), MASK_BIAS, F32)
        vp_ref[:, :DIL_HEAD_DIM] = v_ref[...]
        vp_ref[:, DIL_HEAD_DIM:] = jnp.ones((vp_ref.shape[0], V_PAD - DIL_HEAD_DIM), BF16)

    for u in range(DIL_SUB):
        rows = slice(u * DIL_T, (u + 1) * DIL_T)
        q = q_ref[rows, :]
        scores, starts = [], []
        for r in range(DIL_NCHUNK):
            kc = i * DIL_SUB + u + (r - DIL_SIDE)
            inside = (kc >= 0) & (kc < nblk)
            ks = pl.multiple_of(jnp.clip(kc, 0, nblk - 1) * DIL_T, DIL_T)
            bias = b_ref[jnp.where(inside, r, DIL_NCHUNK)]
            scores.append(lax.dot_general(q, k_ref[pl.ds(ks, DIL_T), :], _NT, preferred_element_type=F32) + bias)
            starts.append(ks)
        m = functools.reduce(jnp.maximum, [jnp.max(s, axis=-1, keepdims=True) for s in scores])
        acc = None
        for s, ks in zip(scores, starts):
            pv = jnp.dot(jnp.exp2(s - m).astype(BF16), vp_ref[pl.ds(ks, DIL_T), :], preferred_element_type=F32)
            acc = pv if acc is None else acc + pv
        o_ref[rows, :] = _normalize(acc).astype(o_ref.dtype)


def _dil_attn(proj, slopes):
    s = proj.shape[0]
    nblk = s // DIL_T
    kern = functools.partial(_dil_attn_kernel, nblk=nblk)
    hd = DIL_HEAD_DIM
    return pl.pallas_call(
        kern,
        grid=(DIL_HEADS, nblk // DIL_SUB),
        in_specs=[
            pl.BlockSpec(memory_space=pltpu.SMEM),
            pl.BlockSpec((DIL_T * DIL_SUB, hd), lambda h, i: (i, COL_QB // hd + h)),
            pl.BlockSpec((s, hd), lambda h, i: (0, COL_KB // hd + h)),
            pl.BlockSpec((s, hd), lambda h, i: (0, COL_VB // hd + h)),
        ],
        out_specs=pl.BlockSpec((DIL_T * DIL_SUB, hd), lambda h, i: (i, h)),
        out_shape=jax.ShapeDtypeStruct((s, DIL_WIDTH), BF16),
        scratch_shapes=[
            pltpu.VMEM((DIL_NCHUNK + 1, DIL_T, DIL_T), F32),
            pltpu.VMEM((s, V_PAD), BF16),
        ],
        compiler_params=_params("arbitrary", "arbitrary"),
        name="dil_attn",
    )(slopes, proj, proj, proj)


def _gated_norm(a_ref, gate_ref, g_ref, y_ref):
    for r in range(0, a_ref.shape[0], ROW_CHUNK):
        rows = slice(r, r + ROW_CHUNK)
        gate = gate_ref[rows, :].astype(F32)
        y = _rms(a_ref[rows, :].astype(F32), g_ref[...]) * (gate * jax.nn.sigmoid(gate))
        y_ref[rows, :] = y.astype(y_ref.dtype)


def _tail_kernel(aa_ref, ab_ref, ga_ref, gb_ref, gma_ref, gmb_ref, x_ref, w_ref, gple_ref, p_ref, wp_ref,
                 gfin_ref, o_ref, yh_ref, *, nj, bn, last_layer):
    jj = pl.program_id(1)

    @pl.when(jj == 0)
    def _():
        _gated_norm(aa_ref, ga_ref, gma_ref, yh_ref.at[:, :MLA_WIDTH])
        _gated_norm(ab_ref, gb_ref, gmb_ref, yh_ref.at[:, MLA_WIDTH:])

    @pl.when(jj < nj)
    def _():
        cols = pl.ds(pl.multiple_of(jj * bn, bn), bn)
        o_ref[:, cols] = x_ref[...] + jnp.dot(yh_ref[...], w_ref[...], preferred_element_type=F32)

    @pl.when(jj == nj)
    def _():
        _rms_rows(o_ref, gple_ref, yh_ref)

    @pl.when(jj >= nj)
    def _():
        cols = pl.ds(pl.multiple_of((jj - nj) * bn, bn), bn)
        gate = jax.nn.sigmoid(jnp.dot(yh_ref[...], w_ref[...], preferred_element_type=F32))
        ple = jnp.dot(p_ref[...].astype(BF16), wp_ref[...], preferred_element_type=F32)
        o_ref[:, cols] = o_ref[:, cols] + ple * gate

    if last_layer:
        @pl.when(jj == 2 * nj - 1)
        def _():
            _rms_rows(o_ref, gfin_ref, o_ref)


def _tail(attn_a, attn_b, proj, g_a, g_b, x, w_cat, g_ple, p, w_ple, g_fin, last_layer, bm=512, bn=512):
    s, d = x.shape
    assert d == MLA_WIDTH + DIL_WIDTH
    c = p.shape[1]
    nj = d // bn
    kern = functools.partial(_tail_kernel, nj=nj, bn=bn, last_layer=last_layer)
    first = lambda i, jj: (i, jnp.minimum(jj, nj - 1))
    second = lambda i, jj: (0, jnp.maximum(jj - nj, 0))
    row = lambda i, jj: (i, 0)
    vec = lambda i, jj: (0, 0)
    return pl.pallas_call(
        kern,
        grid=(s // bm, 2 * nj),
        in_specs=[
            pl.BlockSpec((bm, MLA_WIDTH), row),
            pl.BlockSpec((bm, DIL_WIDTH), row),
            pl.BlockSpec((bm, MLA_WIDTH), lambda i, jj: (i, COL_GATE_A // MLA_WIDTH)),
            pl.BlockSpec((bm, DIL_WIDTH), lambda i, jj: (i, COL_GATE_B // DIL_WIDTH)),
            pl.BlockSpec((1, MLA_WIDTH), vec),
            pl.BlockSpec((1, DIL_WIDTH), vec),
            pl.BlockSpec((bm, bn), first),
            pl.BlockSpec((d, bn), lambda i, jj: (jj // nj, jj % nj)),
            pl.BlockSpec((1, d), vec),
            pl.BlockSpec((bm, c), row),
            pl.BlockSpec((c, bn), second),
            pl.BlockSpec((1, d), vec),
        ],
        out_specs=pl.BlockSpec((bm, d), row),
        out_shape=jax.ShapeDtypeStruct((s, d), F32),
        scratch_shapes=[pltpu.VMEM((bm, d), BF16)],
        compiler_params=_params("parallel", "arbitrary"),
        name="tail",
    )(attn_a, attn_b, proj, proj, g_a, g_b, x, w_cat, g_ple, p, w_ple, g_fin)


def _stack_cast_kernel(a_ref, b_ref, o_ref, *, nb):
    t = pl.program_id(0)

    @pl.when(t < nb)
    def _():
        o_ref[...] = a_ref[...].astype(o_ref.dtype)

    @pl.when(t >= nb)
    def _():
        o_ref[...] = b_ref[...].astype(o_ref.dtype)


def _stack_cast(a, b, rb=512):
    r, c = a.shape
    assert a.shape == b.shape and r % rb == 0
    nb = r // rb
    return pl.pallas_call(
        functools.partial(_stack_cast_kernel, nb=nb),
        grid=(2 * nb,),
        in_specs=[
            pl.BlockSpec((rb, c), lambda t: (jnp.minimum(t, nb - 1), 0)),
            pl.BlockSpec((rb, c), lambda t: (jnp.maximum(t - nb, 0), 0)),
        ],
        out_specs=pl.BlockSpec((rb, c), lambda t: (t, 0)),
        out_shape=jax.ShapeDtypeStruct((2 * r, c), BF16),
        compiler_params=_params("arbitrary"),
        name="stack_cast",
    )(a, b)


def _prep_w_in(w):
    wt = w.T
    d = wt.shape[1]
    lat = Q_LORA + KV_LORA
    head = lat + QK_ROPE
    row_scale = np.ones((wt.shape[0] - head, 1), np.float32)
    row_scale[COL_QB:COL_KB] = DIL_HEAD_DIM ** -0.5 * LOG2E
    w_main = (wt[head:] * row_scale).astype(BF16)
    small = lax.optimization_barrier(wt[:head])
    w_tail = jnp.pad(small, ((0, PROJ_WIDTH - COL_KR - QK_ROPE), (0, 0))).astype(BF16)
    return w_main, w_tail


def _prep_w_uq(w):
    r = w.shape[0]
    w = w.reshape(r, MLA_HEADS, QK_NOPE + QK_ROPE) * ((QK_NOPE + QK_ROPE) ** -0.5 * LOG2E)
    w = jnp.pad(w, ((0, 0), (0, 0), (0, QK_PAD - QK_NOPE - QK_ROPE)))
    return w.reshape(r, MLA_HEADS * QK_PAD).astype(BF16)


def _rope_tables(positions):
    inv = ROPE_THETA ** (-jnp.arange(ROPE_HALF, dtype=F32) / ROPE_HALF)
    ang = positions.astype(F32)[:, None] * inv
    cos, sin = jnp.cos(ang), jnp.sin(ang)
    zero = jnp.zeros_like(cos)
    return (jnp.concatenate([cos, cos, zero, zero], axis=1),
            jnp.concatenate([-sin, sin, zero, zero], axis=1))


def kernel(x, p, positions, g_mix, w_in, g_q_latent, w_uq, g_kv_latent, w_ukv, g_out_mla, g_out_dil,
           w_out, w_ple, g_ple, w_ple_gate, g_final):
    b, s, d = x.shape
    assert b == 1 and s % (DIL_T * DIL_SUB) == 0
    slopes = jnp.asarray(2.0 ** (-ALIBI_MAX_BIAS * np.arange(1, DIL_HEADS + 1) / DIL_HEADS), F32)
    x2d = x.reshape(s, d)
    cos, sin = _rope_tables(positions[0])
    depth = g_mix.shape[0]
    for i in range(depth):
        proj = _in_proj(x2d, g_mix[i][None], *_prep_w_in(w_in[i]))
        q_a, k_a, v_a = _mla_prep(proj, g_q_latent[i][None], g_kv_latent[i][None], cos, sin,
                                  _prep_w_uq(w_uq[i]), w_ukv[i].astype(BF16))
        attn_a = _mla_attn(q_a, k_a, v_a)
        attn_b = _dil_attn(proj, slopes)
        w_cat = _stack_cast(w_out[i], w_ple_gate[i])
        x2d = _tail(attn_a, attn_b, proj, g_out_mla[i][None], g_out_dil[i][None], x2d, w_cat,
                    g_ple[i][None], p[i, 0], w_ple[i].astype(BF16), g_final[None], i == depth - 1)
    return x2d.reshape(b, s, d)
```

```python
import functools

import numpy as np
import jax
import jax.numpy as jnp
from jax import lax
from jax.experimental import pallas as pl
from jax.experimental.pallas import tpu as pltpu

F32 = jnp.float32
BF16 = jnp.bfloat16

EPS = 1e-6
ROPE_THETA = 10000.0
ALIBI_MAX_BIAS = 8.0

MLA_HEADS = 16
Q_LORA = 1024
KV_LORA = 512
QK_NOPE = 128
QK_ROPE = 64
V_HEAD = 128
MLA_WIDTH = MLA_HEADS * V_HEAD
DIL_HEADS = 16
DIL_HEAD_DIM = 128
DIL_WIDTH = DIL_HEADS * DIL_HEAD_DIM
DIL_PATTERNS = ((128, 1), (512, 4), (2048, 16))

LANES = 128
QK_PAD = 256
V_PAD = 2 * LANES
ROPE_HALF = QK_ROPE // 2
LOG2E = float(np.log2(np.e))

COL_GATE_A = 0
COL_QB = COL_GATE_A + MLA_WIDTH
COL_KB = COL_QB + DIL_WIDTH
COL_VB = COL_KB + DIL_WIDTH
COL_GATE_B = COL_VB + DIL_WIDTH
COL_CQ = COL_GATE_B + DIL_WIDTH
COL_CKV = COL_CQ + Q_LORA
COL_KR = COL_CKV + KV_LORA
PROJ_WIDTH = 12288

MASK_BIAS = -1e30

ROW_CHUNK = 64

VMEM_LIMIT = 60 * 1024 * 1024


def _params(*sem):
    return pltpu.CompilerParams(dimension_semantics=sem, vmem_limit_bytes=VMEM_LIMIT)


def _rms(x, g):
    return x * lax.rsqrt(jnp.mean(x * x, axis=-1, keepdims=True) + EPS) * g


def _rms_rows(x_ref, g_ref, h_ref):
    for r in range(0, x_ref.shape[0], ROW_CHUNK):
        rows = slice(r, r + ROW_CHUNK)
        h_ref[rows, :] = _rms(x_ref[rows, :].astype(F32), g_ref[...]).astype(h_ref.dtype)


_NT = (((1,), (1,)), ((), ()))


def _in_proj_kernel(x_ref, g_ref, wm_ref, wt_ref, o_ref, h_ref, *, n_main):
    j = pl.program_id(1)

    @pl.when(j == 0)
    def _():
        _rms_rows(x_ref, g_ref, h_ref)

    @pl.when(j < n_main)
    def _():
        o_ref[...] = lax.dot_general(h_ref[...], wm_ref[...], _NT, preferred_element_type=F32).astype(o_ref.dtype)

    @pl.when(j >= n_main)
    def _():
        o_ref[...] = lax.dot_general(h_ref[...], wt_ref[...], _NT, preferred_element_type=F32).astype(o_ref.dtype)


def _in_proj(x, g, w_main, w_tail, bm=512, bn=1024):
    s, d = x.shape
    n_main = w_main.shape[0] // bn
    n_tail = w_tail.shape[0] // bn
    kern = functools.partial(_in_proj_kernel, n_main=n_main)
    return pl.pallas_call(
        kern,
        grid=(s // bm, n_main + n_tail),
        in_specs=[
            pl.BlockSpec((bm, d), lambda i, j: (i, 0)),
            pl.BlockSpec((1, d), lambda i, j: (0, 0)),
            pl.BlockSpec((bn, d), lambda i, j: (jnp.minimum(j, n_main - 1), 0)),
            pl.BlockSpec((bn, d), lambda i, j: (jnp.maximum(j - n_main, 0), 0)),
        ],
        out_specs=pl.BlockSpec((bm, bn), lambda i, j: (i, j)),
        out_shape=jax.ShapeDtypeStruct((s, (n_main + n_tail) * bn), BF16),
        scratch_shapes=[pltpu.VMEM((bm, d), BF16)],
        compiler_params=_params("parallel", "arbitrary"),
        name="in_proj",
    )(x, g, w_main, w_tail)


def _rope(t, cos, sin):
    lane = lax.broadcasted_iota(jnp.int32, t.shape, 1)
    swapped = jnp.where(lane < ROPE_HALF, pltpu.roll(t, LANES - ROPE_HALF, axis=1), pltpu.roll(t, ROPE_HALF, axis=1))
    return t * cos + swapped * sin


def _mla_prep_kernel(cq_ref, ckv_ref, kr_ref, gq_ref, gkv_ref, cos_ref, sin_ref, wq_ref, wkv_ref,
                     q_out, k_out, v_out, cqn_ref, ckvn_ref, kpe_ref, *, heads):
    @pl.when(pl.program_id(1) == 0)
    def _():
        cqn_ref[...] = _rms(cq_ref[...].astype(F32), gq_ref[...]).astype(BF16)
        ckvn_ref[...] = _rms(ckv_ref[...].astype(F32), gkv_ref[...]).astype(BF16)
        kpe_ref[...] = _rope(kr_ref[...].astype(F32), cos_ref[...], sin_ref[...]).astype(BF16)

    q = jnp.dot(cqn_ref[...], wq_ref[...], preferred_element_type=F32)
    kv = jnp.dot(ckvn_ref[...], wkv_ref[...], preferred_element_type=F32)
    cos = cos_ref[...]
    sin = sin_ref[...]
    for g in range(heads):
        lo = g * QK_PAD
        mid = lo + LANES
        hi = lo + QK_PAD
        q_out[:, lo:mid] = q[:, lo:mid].astype(BF16)
        q_out[:, mid:hi] = _rope(q[:, mid:hi], cos, sin).astype(BF16)
        k_out[:, lo:mid] = kv[:, lo:mid].astype(BF16)
        k_out[:, mid:hi] = kpe_ref[...]
        v_out[:, lo:mid] = kv[:, mid:hi].astype(BF16)
        v_out[:, mid:hi] = jnp.ones((v_out.shape[0], LANES), BF16)


def _mla_prep(proj, gq, gkv, cos, sin, wq, wkv, bm=256, heads=MLA_HEADS):
    s = proj.shape[0]
    bq = heads * QK_PAD
    kern = functools.partial(_mla_prep_kernel, heads=heads)
    return pl.pallas_call(
        kern,
        grid=(s // bm, MLA_HEADS // heads),
        in_specs=[
            pl.BlockSpec((bm, Q_LORA), lambda i, j: (i, COL_CQ // Q_LORA)),
            pl.BlockSpec((bm, KV_LORA), lambda i, j: (i, COL_CKV // KV_LORA)),
            pl.BlockSpec((bm, LANES), lambda i, j: (i, COL_KR // LANES)),
            pl.BlockSpec((1, Q_LORA), lambda i, j: (0, 0)),
            pl.BlockSpec((1, KV_LORA), lambda i, j: (0, 0)),
            pl.BlockSpec((bm, LANES), lambda i, j: (i, 0)),
            pl.BlockSpec((bm, LANES), lambda i, j: (i, 0)),
            pl.BlockSpec((Q_LORA, bq), lambda i, j: (0, j)),
            pl.BlockSpec((KV_LORA, bq), lambda i, j: (0, j)),
        ],
        out_specs=[
            pl.BlockSpec((bm, bq), lambda i, j: (i, j)),
            pl.BlockSpec((bm, bq), lambda i, j: (i, j)),
            pl.BlockSpec((bm, heads * V_PAD), lambda i, j: (i, j)),
        ],
        out_shape=[
            jax.ShapeDtypeStruct((s, MLA_HEADS * QK_PAD), BF16),
            jax.ShapeDtypeStruct((s, MLA_HEADS * QK_PAD), BF16),
            jax.ShapeDtypeStruct((s, MLA_HEADS * V_PAD), BF16),
        ],
        scratch_shapes=[
            pltpu.VMEM((bm, Q_LORA), BF16),
            pltpu.VMEM((bm, KV_LORA), BF16),
            pltpu.VMEM((bm, LANES), BF16),
        ],
        compiler_params=_params("parallel", "arbitrary"),
        name="mla_prep",
    )(proj, proj, proj, gq, gkv, cos, sin, wq, wkv)


def _normalize(acc):
    return acc[:, :V_HEAD] / acc[:, V_HEAD:V_HEAD + 1]


def _mla_attn_kernel(q_ref, k_ref, v_ref, o_ref, *, tq, tk):
    for u in range(q_ref.shape[0] // tq):
        q_rows = slice(u * tq, (u + 1) * tq)
        q = q_ref[q_rows, :]
        m = acc = None
        for c in range(k_ref.shape[0] // tk):
            rows = slice(c * tk, (c + 1) * tk)
            s = lax.dot_general(q, k_ref[rows, :], _NT, preferred_element_type=F32)
            m_c = jnp.max(s, axis=-1, keepdims=True)
            if c == 0:
                m = m_c
                acc = jnp.dot(jnp.exp2(s - m).astype(BF16), v_ref[rows, :], preferred_element_type=F32)
            else:
                m_new = jnp.maximum(m, m_c)
                pv = jnp.dot(jnp.exp2(s - m_new).astype(BF16), v_ref[rows, :], preferred_element_type=F32)
                acc = jnp.exp2(m - m_new) * acc + pv
                m = m_new
        o_ref[q_rows, :] = _normalize(acc).astype(o_ref.dtype)


def _mla_attn(q, k, v, tq=512, sub=8, tk=1024):
    s = q.shape[0]
    kern = functools.partial(_mla_attn_kernel, tq=tq, tk=tk)
    return pl.pallas_call(
        kern,
        grid=(MLA_HEADS, s // (tq * sub)),
        in_specs=[
            pl.BlockSpec((tq * sub, QK_PAD), lambda h, i: (i, h)),
            pl.BlockSpec((s, QK_PAD), lambda h, i: (0, h)),
            pl.BlockSpec((s, V_PAD), lambda h, i: (0, h)),
        ],
        out_specs=pl.BlockSpec((tq * sub, V_HEAD), lambda h, i: (i, h)),
        out_shape=jax.ShapeDtypeStruct((s, MLA_WIDTH), BF16),
        compiler_params=_params("parallel", "arbitrary"),
        name="mla_attn",
    )(q, k, v)


DIL_T = 256
DIL_SUB = 16
DIL_REACH = max(w // 2 for w, _ in DIL_PATTERNS)
DIL_SIDE = DIL_REACH // DIL_T
DIL_NCHUNK = 2 * DIL_SIDE + 1


def _dil_bias(r, slope):
    row = lax.broadcasted_iota(jnp.int32, (DIL_T, DIL_T), 0)
    col = lax.broadcasted_iota(jnp.int32, (DIL_T, DIL_T), 1)
    d = (r - DIL_SIDE) * DIL_T + col - row
    ad = jnp.abs(d)
    cnt = jnp.zeros((DIL_T, DIL_T), F32)
    for window, dil in DIL_PATTERNS:
        member = ((d & (dil - 1)) == 0) & (ad <= window // 2)
        cnt = cnt + member.astype(F32)
    bias = jnp.log2(jnp.maximum(cnt, 1.0)) - (slope * LOG2E) * ad.astype(F32)
    return jnp.where(cnt > 0.0, bias, MASK_BIAS)


def _dil_attn_kernel(slope_ref, q_ref, k_ref, v_ref, o_ref, b_ref, vp_ref, *, nblk):
    h = pl.program_id(0)
    i = pl.program_id(1)

    @pl.when(i == 0)
    def _():
        slope = slope_ref[h]
        for r in range(DIL_NCHUNK):
            b_ref[r] = _dil_bias(r, slope)
        b_ref[DIL_NCHUNK] = jnp.full((DIL_T, DIL_T), MASK_BIAS, F32)
        vp_ref[:, :DIL_HEAD_DIM] = v_ref[...]
        vp_ref[:, DIL_HEAD_DIM:] = jnp.ones((vp_ref.shape[0], V_PAD - DIL_HEAD_DIM), BF16)

    for u in range(DIL_SUB):
        rows = slice(u * DIL_T, (u + 1) * DIL_T)
        q = q_ref[rows, :]
        scores, starts = [], []
        for r in range(DIL_NCHUNK):
            kc = i * DIL_SUB + u + (r - DIL_SIDE)
            inside = (kc >= 0) & (kc < nblk)
            ks = pl.multiple_of(jnp.clip(kc, 0, nblk - 1) * DIL_T, DIL_T)
            bias = b_ref[jnp.where(inside, r, DIL_NCHUNK)]
            scores.append(lax.dot_general(q, k_ref[pl.ds(ks, DIL_T), :], _NT, preferred_element_type=F32) + bias)
            starts.append(ks)
        m = functools.reduce(jnp.maximum, [jnp.max(s, axis=-1, keepdims=True) for s in scores])
        acc = None
        for s, ks in zip(scores, starts):
            pv = jnp.dot(jnp.exp2(s - m).astype(BF16), vp_ref[pl.ds(ks, DIL_T), :], preferred_element_type=F32)
            acc = pv if acc is None else acc + pv
        o_ref[rows, :] = _normalize(acc).astype(o_ref.dtype)


def _dil_attn(proj, slopes):
    s = proj.shape[0]
    nblk = s // DIL_T
    kern = functools.partial(_dil_attn_kernel, nblk=nblk)
    hd = DIL_HEAD_DIM
    return pl.pallas_call(
        kern,
        grid=(DIL_HEADS, nblk // DIL_SUB),
        in_specs=[
            pl.BlockSpec(memory_space=pltpu.SMEM),
            pl.BlockSpec((DIL_T * DIL_SUB, hd), lambda h, i: (i, COL_QB // hd + h)),
            pl.BlockSpec((s, hd), lambda h, i: (0, COL_KB // hd + h)),
            pl.BlockSpec((s, hd), lambda h, i: (0, COL_VB // hd + h)),
        ],
        out_specs=pl.BlockSpec((DIL_T * DIL_SUB, hd), lambda h, i: (i, h)),
        out_shape=jax.ShapeDtypeStruct((s, DIL_WIDTH), BF16),
        scratch_shapes=[
            pltpu.VMEM((DIL_NCHUNK + 1, DIL_T, DIL_T), F32),
            pltpu.VMEM((s, V_PAD), BF16),
        ],
        compiler_params=_params("arbitrary", "arbitrary"),
        name="dil_attn",
    )(slopes, proj, proj, proj)


def _gated_norm(a_ref, gate_ref, g_ref, y_ref):
    for r in range(0, a_ref.shape[0], ROW_CHUNK):
        rows = slice(r, r + ROW_CHUNK)
        gate = gate_ref[rows, :].astype(F32)
        y = _rms(a_ref[rows, :].astype(F32), g_ref[...]) * (gate * jax.nn.sigmoid(gate))
        y_ref[rows, :] = y.astype(y_ref.dtype)


def _tail_kernel(aa_ref, ab_ref, ga_ref, gb_ref, gma_ref, gmb_ref, x_ref, w_ref, gple_ref, p_ref, wp_ref,
                 gfin_ref, o_ref, yh_ref, *, nj, bn, last_layer):
    jj = pl.program_id(1)

    @pl.when(jj == 0)
    def _():
        _gated_norm(aa_ref, ga_ref, gma_ref, yh_ref.at[:, :MLA_WIDTH])
        _gated_norm(ab_ref, gb_ref, gmb_ref, yh_ref.at[:, MLA_WIDTH:])

    @pl.when(jj < nj)
    def _():
        cols = pl.ds(pl.multiple_of(jj * bn, bn), bn)
        o_ref[:, cols] = x_ref[...] + jnp.dot(yh_ref[...], w_ref[...], preferred_element_type=F32)

    @pl.when(jj == nj)
    def _():
        _rms_rows(o_ref, gple_ref, yh_ref)

    @pl.when(jj >= nj)
    def _():
        cols = pl.ds(pl.multiple_of((jj - nj) * bn, bn), bn)
        gate = jax.nn.sigmoid(jnp.dot(yh_ref[...], w_ref[...], preferred_element_type=F32))
        ple = jnp.dot(p_ref[...].astype(BF16), wp_ref[...], preferred_element_type=F32)
        o_ref[:, cols] = o_ref[:, cols] + ple * gate

    if last_layer:
        @pl.when(jj == 2 * nj - 1)
        def _():
            _rms_rows(o_ref, gfin_ref, o_ref)


def _tail(attn_a, attn_b, proj, g_a, g_b, x, w_cat, g_ple, p, w_ple, g_fin, last_layer, bm=512, bn=512):
    s, d = x.shape
    assert d == MLA_WIDTH + DIL_WIDTH
    c = p.shape[1]
    nj = d // bn
    kern = functools.partial(_tail_kernel, nj=nj, bn=bn, last_layer=last_layer)
    first = lambda i, jj: (i, jnp.minimum(jj, nj - 1))
    second = lambda i, jj: (0, jnp.maximum(jj - nj, 0))
    row = lambda i, jj: (i, 0)
    vec = lambda i, jj: (0, 0)
    return pl.pallas_call(
        kern,
        grid=(s // bm, 2 * nj),
        in_specs=[
            pl.BlockSpec((bm, MLA_WIDTH), row),
            pl.BlockSpec((bm, DIL_WIDTH), row),
            pl.BlockSpec((bm, MLA_WIDTH), lambda i, jj: (i, COL_GATE_A // MLA_WIDTH)),
            pl.BlockSpec((bm, DIL_WIDTH), lambda i, jj: (i, COL_GATE_B // DIL_WIDTH)),
            pl.BlockSpec((1, MLA_WIDTH), vec),
            pl.BlockSpec((1, DIL_WIDTH), vec),
            pl.BlockSpec((bm, bn), first),
            pl.BlockSpec((d, bn), lambda i, jj: (jj // nj, jj % nj)),
            pl.BlockSpec((1, d), vec),
            pl.BlockSpec((bm, c), row),
            pl.BlockSpec((c, bn), second),
            pl.BlockSpec((1, d), vec),
        ],
        out_specs=pl.BlockSpec((bm, d), row),
        out_shape=jax.ShapeDtypeStruct((s, d), F32),
        scratch_shapes=[pltpu.VMEM((bm, d), BF16)],
        compiler_params=_params("parallel", "arbitrary"),
        name="tail",
    )(attn_a, attn_b, proj, proj, g_a, g_b, x, w_cat, g_ple, p, w_ple, g_fin)


def _stack_cast_kernel(a_ref, b_ref, o_ref, *, nb):
    t = pl.program_id(0)

    @pl.when(t < nb)
    def _():
        o_ref[...] = a_ref[...].astype(o_ref.dtype)

    @pl.when(t >= nb)
    def _():
        o_ref[...] = b_ref[...].astype(o_ref.dtype)


def _stack_cast(a, b, rb=512):
    r, c = a.shape
    assert a.shape == b.shape and r % rb == 0
    nb = r // rb
    return pl.pallas_call(
        functools.partial(_stack_cast_kernel, nb=nb),
        grid=(2 * nb,),
        in_specs=[
            pl.BlockSpec((rb, c), lambda t: (jnp.minimum(t, nb - 1), 0)),
            pl.BlockSpec((rb, c), lambda t: (jnp.maximum(t - nb, 0), 0)),
        ],
        out_specs=pl.BlockSpec((rb, c), lambda t: (t, 0)),
        out_shape=jax.ShapeDtypeStruct((2 * r, c), BF16),
        compiler_params=_params("arbitrary"),
        name="stack_cast",
    )(a, b)


def _prep_w_in(w):
    wt = w.T
    d = wt.shape[1]
    lat = Q_LORA + KV_LORA
    head = lat + QK_ROPE
    row_scale = np.ones((wt.shape[0] - head, 1), np.float32)
    row_scale[COL_QB:COL_KB] = DIL_HEAD_DIM ** -0.5 * LOG2E
    w_main = (wt[head:] * row_scale).astype(BF16)
    small = lax.optimization_barrier(wt[:head])
    w_tail = jnp.pad(small, ((0, PROJ_WIDTH - COL_KR - QK_ROPE), (0, 0))).astype(BF16)
    return w_main, w_tail


def _prep_w_uq(w):
    r = w.shape[0]
    w = w.reshape(r, MLA_HEADS, QK_NOPE + QK_ROPE) * ((QK_NOPE + QK_ROPE) ** -0.5 * LOG2E)
    w = jnp.pad(w, ((0, 0), (0, 0), (0, QK_PAD - QK_NOPE - QK_ROPE)))
    return w.reshape(r, MLA_HEADS * QK_PAD).astype(BF16)


def _rope_tables(positions):
    inv = ROPE_THETA ** (-jnp.arange(ROPE_HALF, dtype=F32) / ROPE_HALF)
    ang = positions.astype(F32)[:, None] * inv
    cos, sin = jnp.cos(ang), jnp.sin(ang)
    zero = jnp.zeros_like(cos)
    return (jnp.concatenate([cos, cos, zero, zero], axis=1),
            jnp.concatenate([-sin, sin, zero, zero], axis=1))


def kernel(x, p, positions, g_mix, w_in, g_q_latent, w_uq, g_kv_latent, w_ukv, g_out_mla, g_out_dil,
           w_out, w_ple, g_ple, w_ple_gate, g_final):
    b, s, d = x.shape
    assert b == 1 and s % (DIL_T * DIL_SUB) == 0
    slopes = jnp.asarray(2.0 ** (-ALIBI_MAX_BIAS * np.arange(1, DIL_HEADS + 1) / DIL_HEADS), F32)
    x2d = x.reshape(s, d)
    cos, sin = _rope_tables(positions[0])
    depth = g_mix.shape[0]
    for i in range(depth):
        proj = _in_proj(x2d, g_mix[i][None], *_prep_w_in(w_in[i]))
        q_a, k_a, v_a = _mla_prep(proj, g_q_latent[i][None], g_kv_latent[i][None], cos, sin,
                                  _prep_w_uq(w_uq[i]), w_ukv[i].astype(BF16))
        attn_a = _mla_attn(q_a, k_a, v_a)
        attn_b = _dil_attn(proj, slopes)
        w_cat = _stack_cast(w_out[i], w_ple_gate[i])
        x2d = _tail(attn_a, attn_b, proj, g_out_mla[i][None], g_out_dil[i][None], x2d, w_cat,
                    g_ple[i][None], p[i, 0], w_ple[i].astype(BF16), g_final[None], i == depth - 1)
    return x2d.reshape(b, s, d)
```

```python
import functools

import numpy as np
import jax
import jax.numpy as jnp
from jax import lax
from jax.experimental import pallas as pl
from jax.experimental.pallas import tpu as pltpu

F32 = jnp.float32
BF16 = jnp.bfloat16

EPS = 1e-6
ROPE_THETA = 10000.0
ALIBI_MAX_BIAS = 8.0

MLA_HEADS = 16
Q_LORA = 1024
KV_LORA = 512
QK_NOPE = 128
QK_ROPE = 64
V_HEAD = 128
MLA_WIDTH = MLA_HEADS * V_HEAD
DIL_HEADS = 16
DIL_HEAD_DIM = 128
DIL_WIDTH = DIL_HEADS * DIL_HEAD_DIM
DIL_PATTERNS = ((128, 1), (512, 4), (2048, 16))

LANES = 128
QK_PAD = 256
V_PAD = 2 * LANES
ROPE_HALF = QK_ROPE // 2
LOG2E = float(np.log2(np.e))

COL_GATE_A = 0
COL_QB = COL_GATE_A + MLA_WIDTH
COL_KB = COL_QB + DIL_WIDTH
COL_VB = COL_KB + DIL_WIDTH
COL_GATE_B = COL_VB + DIL_WIDTH
COL_CQ = COL_GATE_B + DIL_WIDTH
COL_CKV = COL_CQ + Q_LORA
COL_KR = COL_CKV + KV_LORA
PROJ_WIDTH = 12288

MASK_BIAS = -1e30

ROW_CHUNK = 64

VMEM_LIMIT = 60 * 1024 * 1024


def _params(*sem):
    return pltpu.CompilerParams(dimension_semantics=sem, vmem_limit_bytes=VMEM_LIMIT)


def _rms(x, g):
    return x * lax.rsqrt(jnp.mean(x * x, axis=-1, keepdims=True) + EPS) * g


def _rms_rows(x_ref, g_ref, h_ref):
    for r in range(0, x_ref.shape[0], ROW_CHUNK):
        rows = slice(r, r + ROW_CHUNK)
        h_ref[rows, :] = _rms(x_ref[rows, :].astype(F32), g_ref[...]).astype(h_ref.dtype)


_NT = (((1,), (1,)), ((), ()))


def _in_proj_kernel(x_ref, g_ref, wm_ref, wt_ref, o_ref, h_ref, *, parts, n_main):
    j = pl.program_id(1)
    bx = x_ref.shape[0]

    for part in range(parts):
        @pl.when(j == part)
        def _():
            _rms_rows(x_ref, g_ref, h_ref.at[part * bx:(part + 1) * bx, :])

    @pl.when((j >= parts) & (j < parts + n_main))
    def _():
        o_ref[...] = lax.dot_general(h_ref[...], wm_ref[...], _NT, preferred_element_type=F32).astype(o_ref.dtype)

    @pl.when(j >= parts + n_main)
    def _():
        o_ref[...] = lax.dot_general(h_ref[...], wt_ref[...], _NT, preferred_element_type=F32).astype(o_ref.dtype)


def _in_proj(x, g, w_main, w_tail, bx=512, parts=2, bn=512):
    s, d = x.shape
    bm = bx * parts
    n_main = w_main.shape[0] // bn
    n_tail = w_tail.shape[0] // bn
    kern = functools.partial(_in_proj_kernel, parts=parts, n_main=n_main)
    return pl.pallas_call(
        kern,
        grid=(s // bm, parts + n_main + n_tail),
        in_specs=[
            pl.BlockSpec((bx, d), lambda i, j: (i * parts + jnp.minimum(j, parts - 1), 0)),
            pl.BlockSpec((1, d), lambda i, j: (0, 0)),
            pl.BlockSpec((bn, d), lambda i, j: (jnp.clip(j - parts, 0, n_main - 1), 0)),
            pl.BlockSpec((bn, d), lambda i, j: (jnp.clip(j - parts - n_main, 0, n_tail - 1), 0)),
        ],
        out_specs=pl.BlockSpec((bm, bn), lambda i, j: (i, jnp.maximum(j - parts, 0))),
        out_shape=jax.ShapeDtypeStruct((s, (n_main + n_tail) * bn), BF16),
        scratch_shapes=[pltpu.VMEM((bm, d), BF16)],
        compiler_params=_params("parallel", "arbitrary"),
        name="in_proj",
    )(x, g, w_main, w_tail)


def _rope(t, cos, sin):
    lane = lax.broadcasted_iota(jnp.int32, t.shape, 1)
    swapped = jnp.where(lane < ROPE_HALF, pltpu.roll(t, LANES - ROPE_HALF, axis=1), pltpu.roll(t, ROPE_HALF, axis=1))
    return t * cos + swapped * sin


def _mla_prep_kernel(cq_ref, ckv_ref, kr_ref, gq_ref, gkv_ref, cos_ref, sin_ref, wq_ref, wkv_ref,
                     q_out, k_out, v_out, cqn_ref, ckvn_ref, kpe_ref, *, heads):
    @pl.when(pl.program_id(1) == 0)
    def _():
        cqn_ref[...] = _rms(cq_ref[...].astype(F32), gq_ref[...]).astype(BF16)
        ckvn_ref[...] = _rms(ckv_ref[...].astype(F32), gkv_ref[...]).astype(BF16)
        kpe_ref[...] = _rope(kr_ref[...].astype(F32), cos_ref[...], sin_ref[...]).astype(BF16)

    q = jnp.dot(cqn_ref[...], wq_ref[...], preferred_element_type=F32)
    kv = jnp.dot(ckvn_ref[...], wkv_ref[...], preferred_element_type=F32)
    cos = cos_ref[...]
    sin = sin_ref[...]
    for g in range(heads):
        lo = g * QK_PAD
        mid = lo + LANES
        hi = lo + QK_PAD
        q_out[:, lo:mid] = q[:, lo:mid].astype(BF16)
        q_out[:, mid:hi] = _rope(q[:, mid:hi], cos, sin).astype(BF16)
        k_out[:, lo:mid] = kv[:, lo:mid].astype(BF16)
        k_out[:, mid:hi] = kpe_ref[...]
        v_out[:, lo:mid] = kv[:, mid:hi].astype(BF16)
        v_out[:, mid:hi] = jnp.ones((v_out.shape[0], LANES), BF16)


def _mla_prep(proj, gq, gkv, cos, sin, wq, wkv, bm=256, heads=MLA_HEADS):
    s = proj.shape[0]
    bq = heads * QK_PAD
    kern = functools.partial(_mla_prep_kernel, heads=heads)
    return pl.pallas_call(
        kern,
        grid=(s // bm, MLA_HEADS // heads),
        in_specs=[
            pl.BlockSpec((bm, Q_LORA), lambda i, j: (i, COL_CQ // Q_LORA)),
            pl.BlockSpec((bm, KV_LORA), lambda i, j: (i, COL_CKV // KV_LORA)),
            pl.BlockSpec((bm, LANES), lambda i, j: (i, COL_KR // LANES)),
            pl.BlockSpec((1, Q_LORA), lambda i, j: (0, 0)),
            pl.BlockSpec((1, KV_LORA), lambda i, j: (0, 0)),
            pl.BlockSpec((bm, LANES), lambda i, j: (i, 0)),
            pl.BlockSpec((bm, LANES), lambda i, j: (i, 0)),
            pl.BlockSpec((Q_LORA, bq), lambda i, j: (0, j)),
            pl.BlockSpec((KV_LORA, bq), lambda i, j: (0, j)),
        ],
        out_specs=[
            pl.BlockSpec((bm, bq), lambda i, j: (i, j)),
            pl.BlockSpec((bm, bq), lambda i, j: (i, j)),
            pl.BlockSpec((bm, heads * V_PAD), lambda i, j: (i, j)),
        ],
        out_shape=[
            jax.ShapeDtypeStruct((s, MLA_HEADS * QK_PAD), BF16),
            jax.ShapeDtypeStruct((s, MLA_HEADS * QK_PAD), BF16),
            jax.ShapeDtypeStruct((s, MLA_HEADS * V_PAD), BF16),
        ],
        scratch_shapes=[
            pltpu.VMEM((bm, Q_LORA), BF16),
            pltpu.VMEM((bm, KV_LORA), BF16),
            pltpu.VMEM((bm, LANES), BF16),
        ],
        compiler_params=_params("parallel", "arbitrary"),
        name="mla_prep",
    )(proj, proj, proj, gq, gkv, cos, sin, wq, wkv)


def _normalize(acc):
    return acc[:, :V_HEAD] / acc[:, V_HEAD:V_HEAD + 1]


def _mla_attn_kernel(q_ref, k_ref, v_ref, o_ref, *, tq, tk):
    for u in range(q_ref.shape[0] // tq):
        q_rows = slice(u * tq, (u + 1) * tq)
        q = q_ref[q_rows, :]
        m = acc = None
        for c in range(k_ref.shape[0] // tk):
            rows = slice(c * tk, (c + 1) * tk)
            s = lax.dot_general(q, k_ref[rows, :], _NT, preferred_element_type=F32)
            m_c = jnp.max(s, axis=-1, keepdims=True)
            if c == 0:
                m = m_c
                acc = jnp.dot(jnp.exp2(s - m).astype(BF16), v_ref[rows, :], preferred_element_type=F32)
            else:
                m_new = jnp.maximum(m, m_c)
                pv = jnp.dot(jnp.exp2(s - m_new).astype(BF16), v_ref[rows, :], preferred_element_type=F32)
                acc = jnp.exp2(m - m_new) * acc + pv
                m = m_new
        o_ref[q_rows, :] = _normalize(acc).astype(o_ref.dtype)


def _mla_attn(q, k, v, tq=512, sub=4, tk=1024):
    s = q.shape[0]
    kern = functools.partial(_mla_attn_kernel, tq=tq, tk=tk)
    return pl.pallas_call(
        kern,
        grid=(MLA_HEADS, s // (tq * sub)),
        in_specs=[
            pl.BlockSpec((tq * sub, QK_PAD), lambda h, i: (i, h)),
            pl.BlockSpec((s, QK_PAD), lambda h, i: (0, h)),
            pl.BlockSpec((s, V_PAD), lambda h, i: (0, h)),
        ],
        out_specs=pl.BlockSpec((tq * sub, V_HEAD), lambda h, i: (i, h)),
        out_shape=jax.ShapeDtypeStruct((s, MLA_WIDTH), BF16),
        compiler_params=_params("parallel", "arbitrary"),
        name="mla_attn",
    )(q, k, v)


DIL_T = 256
DIL_SUB = 16
DIL_REACH = max(w // 2 for w, _ in DIL_PATTERNS)
DIL_SIDE = DIL_REACH // DIL_T
DIL_NCHUNK = 2 * DIL_SIDE + 1


def _dil_bias(r, slope):
    row = lax.broadcasted_iota(jnp.int32, (DIL_T, DIL_T), 0)
    col = lax.broadcasted_iota(jnp.int32, (DIL_T, DIL_T), 1)
    d = (r - DIL_SIDE) * DIL_T + col - row
    ad = jnp.abs(d)
    cnt = jnp.zeros((DIL_T, DIL_T), F32)
    for window, dil in DIL_PATTERNS:
        member = ((d & (dil - 1)) == 0) & (ad <= window // 2)
        cnt = cnt + member.astype(F32)
    bias = jnp.log2(jnp.maximum(cnt, 1.0)) - (slope * LOG2E) * ad.astype(F32)
    return jnp.where(cnt > 0.0, bias, MASK_BIAS)


def _dil_attn_kernel(slope_ref, q_ref, k_ref, v_ref, o_ref, b_ref, vp_ref, *, nblk):
    h = pl.program_id(0)
    i = pl.program_id(1)

    @pl.when(i == 0)
    def _():
        slope = slope_ref[h]
        for r in range(DIL_NCHUNK):
            b_ref[r] = _dil_bias(r, slope)
        b_ref[DIL_NCHUNK] = jnp.full((DIL_T, DIL_T), MASK_BIAS, F32)
        vp_ref[:, :DIL_HEAD_DIM] = v_ref[...]
        vp_ref[:, DIL_HEAD_DIM:] = jnp.ones((vp_ref.shape[0], V_PAD - DIL_HEAD_DIM), BF16)

    for u in range(DIL_SUB):
        rows = slice(u * DIL_T, (u + 1) * DIL_T)
        q = q_ref[rows, :]
        scores, starts = [], []
        for r in range(DIL_NCHUNK):
            kc = i * DIL_SUB + u + (r - DIL_SIDE)
            inside = (kc >= 0) & (kc < nblk)
            ks = pl.multiple_of(jnp.clip(kc, 0, nblk - 1) * DIL_T, DIL_T)
            bias = b_ref[jnp.where(inside, r, DIL_NCHUNK)]
            scores.append(lax.dot_general(q, k_ref[pl.ds(ks, DIL_T), :], _NT, preferred_element_type=F32) + bias)
            starts.append(ks)
        m = functools.reduce(jnp.maximum, [jnp.max(s, axis=-1, keepdims=True) for s in scores])
        acc = None
        for s, ks in zip(scores, starts):
            pv = jnp.dot(jnp.exp2(s - m).astype(BF16), vp_ref[pl.ds(ks, DIL_T), :], preferred_element_type=F32)
            acc = pv if acc is None else acc + pv
        o_ref[rows, :] = _normalize(acc).astype(o_ref.dtype)


def _dil_attn(proj, slopes):
    s = proj.shape[0]
    nblk = s // DIL_T
    kern = functools.partial(_dil_attn_kernel, nblk=nblk)
    hd = DIL_HEAD_DIM
    return pl.pallas_call(
        kern,
        grid=(DIL_HEADS, nblk // DIL_SUB),
        in_specs=[
            pl.BlockSpec(memory_space=pltpu.SMEM),
            pl.BlockSpec((DIL_T * DIL_SUB, hd), lambda h, i: (i, COL_QB // hd + h)),
            pl.BlockSpec((s, hd), lambda h, i: (0, COL_KB // hd + h)),
            pl.BlockSpec((s, hd), lambda h, i: (0, COL_VB // hd + h)),
        ],
        out_specs=pl.BlockSpec((DIL_T * DIL_SUB, hd), lambda h, i: (i, h)),
        out_shape=jax.ShapeDtypeStruct((s, DIL_WIDTH), BF16),
        scratch_shapes=[
            pltpu.VMEM((DIL_NCHUNK + 1, DIL_T, DIL_T), F32),
            pltpu.VMEM((s, V_PAD), BF16),
        ],
        compiler_params=_params("arbitrary", "arbitrary"),
        name="dil_attn",
    )(slopes, proj, proj, proj)


def _gated_norm(a_ref, gate_ref, g_ref, y_ref):
    for r in range(0, a_ref.shape[0], ROW_CHUNK):
        rows = slice(r, r + ROW_CHUNK)
        gate = gate_ref[rows, :].astype(F32)
        y = _rms(a_ref[rows, :].astype(F32), g_ref[...]) * (gate * jax.nn.sigmoid(gate))
        y_ref[rows, :] = y.astype(y_ref.dtype)


def _tail_kernel(aa_ref, ab_ref, ga_ref, gb_ref, gma_ref, gmb_ref, x_ref, w_ref, gple_ref, p_ref, wp_ref,
                 gfin_ref, o_ref, yh_ref, *, nj, bn, last_layer):
    jj = pl.program_id(1)

    @pl.when(jj == 0)
    def _():
        _gated_norm(aa_ref, ga_ref, gma_ref, yh_ref.at[:, :MLA_WIDTH])
        _gated_norm(ab_ref, gb_ref, gmb_ref, yh_ref.at[:, MLA_WIDTH:])

    @pl.when(jj < nj)
    def _():
        cols = pl.ds(pl.multiple_of(jj * bn, bn), bn)
        o_ref[:, cols] = x_ref[...] + jnp.dot(yh_ref[...], w_ref[...], preferred_element_type=F32)

    @pl.when(jj == nj)
    def _():
        _rms_rows(o_ref, gple_ref, yh_ref)

    @pl.when(jj >= nj)
    def _():
        cols = pl.ds(pl.multiple_of((jj - nj) * bn, bn), bn)
        gate = jax.nn.sigmoid(jnp.dot(yh_ref[...], w_ref[...], preferred_element_type=F32))
        ple = jnp.dot(p_ref[...].astype(BF16), wp_ref[...], preferred_element_type=F32)
        o_ref[:, cols] = o_ref[:, cols] + ple * gate

    if last_layer:
        @pl.when(jj == 2 * nj - 1)
        def _():
            _rms_rows(o_ref, gfin_ref, o_ref)


def _tail(attn_a, attn_b, proj, g_a, g_b, x, w_cat, g_ple, p, w_ple, g_fin, last_layer, bm=512, bn=512):
    s, d = x.shape
    assert d == MLA_WIDTH + DIL_WIDTH
    c = p.shape[1]
    nj = d // bn
    kern = functools.partial(_tail_kernel, nj=nj, bn=bn, last_layer=last_layer)
    first = lambda i, jj: (i, jnp.minimum(jj, nj - 1))
    second = lambda i, jj: (0, jnp.maximum(jj - nj, 0))
    row = lambda i, jj: (i, 0)
    vec = lambda i, jj: (0, 0)
    return pl.pallas_call(
        kern,
        grid=(s // bm, 2 * nj),
        in_specs=[
            pl.BlockSpec((bm, MLA_WIDTH), row),
            pl.BlockSpec((bm, DIL_WIDTH), row),
            pl.BlockSpec((bm, MLA_WIDTH), lambda i, jj: (i, COL_GATE_A // MLA_WIDTH)),
            pl.BlockSpec((bm, DIL_WIDTH), lambda i, jj: (i, COL_GATE_B // DIL_WIDTH)),
            pl.BlockSpec((1, MLA_WIDTH), vec),
            pl.BlockSpec((1, DIL_WIDTH), vec),
            pl.BlockSpec((bm, bn), first),
            pl.BlockSpec((d, bn), lambda i, jj: (jj // nj, jj % nj)),
            pl.BlockSpec((1, d), vec),
            pl.BlockSpec((bm, c), row),
            pl.BlockSpec((c, bn), second),
            pl.BlockSpec((1, d), vec),
        ],
        out_specs=pl.BlockSpec((bm, d), row),
        out_shape=jax.ShapeDtypeStruct((s, d), F32),
        scratch_shapes=[pltpu.VMEM((bm, d), BF16)],
        compiler_params=_params("parallel", "arbitrary"),
        name="tail",
    )(attn_a, attn_b, proj, proj, g_a, g_b, x, w_cat, g_ple, p, w_ple, g_fin)


def _stack_cast_kernel(a_ref, b_ref, o_ref, *, nb):
    t = pl.program_id(0)

    @pl.when(t < nb)
    def _():
        o_ref[...] = a_ref[...].astype(o_ref.dtype)

    @pl.when(t >= nb)
    def _():
        o_ref[...] = b_ref[...].astype(o_ref.dtype)


def _stack_cast(a, b, rb=512):
    r, c = a.shape
    assert a.shape == b.shape and r % rb == 0
    nb = r // rb
    return pl.pallas_call(
        functools.partial(_stack_cast_kernel, nb=nb),
        grid=(2 * nb,),
        in_specs=[
            pl.BlockSpec((rb, c), lambda t: (jnp.minimum(t, nb - 1), 0)),
            pl.BlockSpec((rb, c), lambda t: (jnp.maximum(t - nb, 0), 0)),
        ],
        out_specs=pl.BlockSpec((rb, c), lambda t: (t, 0)),
        out_shape=jax.ShapeDtypeStruct((2 * r, c), BF16),
        compiler_params=_params("arbitrary"),
        name="stack_cast",
    )(a, b)


def _prep_w_in(w):
    wt = w.T
    d = wt.shape[1]
    lat = Q_LORA + KV_LORA
    head = lat + QK_ROPE
    row_scale = np.ones((wt.shape[0] - head, 1), np.float32)
    row_scale[COL_QB:COL_KB] = DIL_HEAD_DIM ** -0.5 * LOG2E
    w_main = (wt[head:] * row_scale).astype(BF16)
    small = lax.optimization_barrier(wt[:head])
    w_tail = jnp.pad(small, ((0, PROJ_WIDTH - COL_KR - QK_ROPE), (0, 0))).astype(BF16)
    return w_main, w_tail


def _prep_w_uq(w):
    r = w.shape[0]
    w = w.reshape(r, MLA_HEADS, QK_NOPE + QK_ROPE) * ((QK_NOPE + QK_ROPE) ** -0.5 * LOG2E)
    w = jnp.pad(w, ((0, 0), (0, 0), (0, QK_PAD - QK_NOPE - QK_ROPE)))
    return w.reshape(r, MLA_HEADS * QK_PAD).astype(BF16)


def _rope_tables(positions):
    inv = ROPE_THETA ** (-jnp.arange(ROPE_HALF, dtype=F32) / ROPE_HALF)
    ang = positions.astype(F32)[:, None] * inv
    cos, sin = jnp.cos(ang), jnp.sin(ang)
    zero = jnp.zeros_like(cos)
    return (jnp.concatenate([cos, cos, zero, zero], axis=1),
            jnp.concatenate([-sin, sin, zero, zero], axis=1))


def kernel(x, p, positions, g_mix, w_in, g_q_latent, w_uq, g_kv_latent, w_ukv, g_out_mla, g_out_dil,
           w_out, w_ple, g_ple, w_ple_gate, g_final):
    b, s, d = x.shape
    assert b == 1 and s % (DIL_T * DIL_SUB) == 0
    slopes = jnp.asarray(2.0 ** (-ALIBI_MAX_BIAS * np.arange(1, DIL_HEADS + 1) / DIL_HEADS), F32)
    x2d = x.reshape(s, d)
    cos, sin = _rope_tables(positions[0])
    depth = g_mix.shape[0]
    for i in range(depth):
        proj = _in_proj(x2d, g_mix[i][None], *_prep_w_in(w_in[i]))
        q_a, k_a, v_a = _mla_prep(proj, g_q_latent[i][None], g_kv_latent[i][None], cos, sin,
                                  _prep_w_uq(w_uq[i]), w_ukv[i].astype(BF16))
        attn_a = _mla_attn(q_a, k_a, v_a)
        attn_b = _dil_attn(proj, slopes)
        w_cat = _stack_cast(w_out[i], w_ple_gate[i])
        x2d = _tail(attn_a, attn_b, proj, g_out_mla[i][None], g_out_dil[i][None], x2d, w_cat,
                    g_ple[i][None], p[i, 0], w_ple[i].astype(BF16), g_final[None], i == depth - 1)
    return x2d.reshape(b, s, d)
```

```python
import functools

import numpy as np
import jax
import jax.numpy as jnp
from jax import lax
from jax.experimental import pallas as pl
from jax.experimental.pallas import tpu as pltpu

F32 = jnp.float32
BF16 = jnp.bfloat16

EPS = 1e-6
ROPE_THETA = 10000.0
ALIBI_MAX_BIAS = 8.0

MLA_HEADS = 16
Q_LORA = 1024
KV_LORA = 512
QK_NOPE = 128
QK_ROPE = 64
V_HEAD = 128
MLA_WIDTH = MLA_HEADS * V_HEAD
DIL_HEADS = 16
DIL_HEAD_DIM = 128
DIL_WIDTH = DIL_HEADS * DIL_HEAD_DIM
DIL_PATTERNS = ((128, 1), (512, 4), (2048, 16))

LANES = 128
QK_PAD = 256
V_PAD = 2 * LANES
ROPE_HALF = QK_ROPE // 2
LOG2E = float(np.log2(np.e))

COL_GATE_A = 0
COL_QB = COL_GATE_A + MLA_WIDTH
COL_KB = COL_QB + DIL_WIDTH
COL_VB = COL_KB + DIL_WIDTH
COL_GATE_B = COL_VB + DIL_WIDTH
COL_CQ = COL_GATE_B + DIL_WIDTH
COL_CKV = COL_CQ + Q_LORA
COL_KR = COL_CKV + KV_LORA
PROJ_WIDTH = 12288

MASK_BIAS = -1e30

ROW_CHUNK = 64

VMEM_LIMIT = 60 * 1024 * 1024


def _params(*sem):
    return pltpu.CompilerParams(dimension_semantics=sem, vmem_limit_bytes=VMEM_LIMIT)


def _rms(x, g):
    return x * lax.rsqrt(jnp.mean(x * x, axis=-1, keepdims=True) + EPS) * g


def _rms_rows(x_ref, g_ref, h_ref):
    for r in range(0, x_ref.shape[0], ROW_CHUNK):
        rows = slice(r, r + ROW_CHUNK)
        h_ref[rows, :] = _rms(x_ref[rows, :].astype(F32), g_ref[...]).astype(h_ref.dtype)


_NT = (((1,), (1,)), ((), ()))


def _in_proj_kernel(x_ref, g_ref, wm_ref, wt_ref, o_ref, h_ref, *, parts, n_main):
    j = pl.program_id(1)
    bx = x_ref.shape[0]

    for part in range(parts):
        @pl.when(j == part)
        def _():
            _rms_rows(x_ref, g_ref, h_ref.at[part * bx:(part + 1) * bx, :])

    @pl.when((j >= parts) & (j < parts + n_main))
    def _():
        o_ref[...] = lax.dot_general(h_ref[...], wm_ref[...], _NT, preferred_element_type=F32).astype(o_ref.dtype)

    @pl.when(j >= parts + n_main)
    def _():
        o_ref[...] = lax.dot_general(h_ref[...], wt_ref[...], _NT, preferred_element_type=F32).astype(o_ref.dtype)


def _in_proj(x, g, w_main, w_tail, bx=512, parts=4, bn=512):
    s, d = x.shape
    bm = bx * parts
    n_main = w_main.shape[0] // bn
    n_tail = w_tail.shape[0] // bn
    kern = functools.partial(_in_proj_kernel, parts=parts, n_main=n_main)
    return pl.pallas_call(
        kern,
        grid=(s // bm, parts + n_main + n_tail),
        in_specs=[
            pl.BlockSpec((bx, d), lambda i, j: (i * parts + jnp.minimum(j, parts - 1), 0)),
            pl.BlockSpec((1, d), lambda i, j: (0, 0)),
            pl.BlockSpec((bn, d), lambda i, j: (jnp.clip(j - parts, 0, n_main - 1), 0)),
            pl.BlockSpec((bn, d), lambda i, j: (jnp.clip(j - parts - n_main, 0, n_tail - 1), 0)),
        ],
        out_specs=pl.BlockSpec((bm, bn), lambda i, j: (i, jnp.maximum(j - parts, 0))),
        out_shape=jax.ShapeDtypeStruct((s, (n_main + n_tail) * bn), BF16),
        scratch_shapes=[pltpu.VMEM((bm, d), BF16)],
        compiler_params=_params("parallel", "arbitrary"),
        name="in_proj",
    )(x, g, w_main, w_tail)


def _rope(t, cos, sin):
    lane = lax.broadcasted_iota(jnp.int32, t.shape, 1)
    swapped = jnp.where(lane < ROPE_HALF, pltpu.roll(t, LANES - ROPE_HALF, axis=1), pltpu.roll(t, ROPE_HALF, axis=1))
    return t * cos + swapped * sin


def _mla_prep_kernel(cq_ref, ckv_ref, kr_ref, gq_ref, gkv_ref, cos_ref, sin_ref, wq_ref, wkv_ref,
                     q_out, k_out, v_out, cqn_ref, ckvn_ref, kpe_ref, *, heads):
    @pl.when(pl.program_id(1) == 0)
    def _():
        cqn_ref[...] = _rms(cq_ref[...].astype(F32), gq_ref[...]).astype(BF16)
        ckvn_ref[...] = _rms(ckv_ref[...].astype(F32), gkv_ref[...]).astype(BF16)
        kpe_ref[...] = _rope(kr_ref[...].astype(F32), cos_ref[...], sin_ref[...]).astype(BF16)

    q = jnp.dot(cqn_ref[...], wq_ref[...], preferred_element_type=F32)
    kv = jnp.dot(ckvn_ref[...], wkv_ref[...], preferred_element_type=F32)
    cos = cos_ref[...]
    sin = sin_ref[...]
    for g in range(heads):
        lo = g * QK_PAD
        mid = lo + LANES
        hi = lo + QK_PAD
        q_out[:, lo:mid] = q[:, lo:mid].astype(BF16)
        q_out[:, mid:hi] = _rope(q[:, mid:hi], cos, sin).astype(BF16)
        k_out[:, lo:mid] = kv[:, lo:mid].astype(BF16)
        k_out[:, mid:hi] = kpe_ref[...]
        v_out[:, lo:mid] = kv[:, mid:hi].astype(BF16)
        v_out[:, mid:hi] = jnp.ones((v_out.shape[0], LANES), BF16)


def _mla_prep(proj, gq, gkv, cos, sin, wq, wkv, bm=256, heads=MLA_HEADS):
    s = proj.shape[0]
    bq = heads * QK_PAD
    kern = functools.partial(_mla_prep_kernel, heads=heads)
    return pl.pallas_call(
        kern,
        grid=(s // bm, MLA_HEADS // heads),
        in_specs=[
            pl.BlockSpec((bm, Q_LORA), lambda i, j: (i, COL_CQ // Q_LORA)),
            pl.BlockSpec((bm, KV_LORA), lambda i, j: (i, COL_CKV // KV_LORA)),
            pl.BlockSpec((bm, LANES), lambda i, j: (i, COL_KR // LANES)),
            pl.BlockSpec((1, Q_LORA), lambda i, j: (0, 0)),
            pl.BlockSpec((1, KV_LORA), lambda i, j: (0, 0)),
            pl.BlockSpec((bm, LANES), lambda i, j: (i, 0)),
            pl.BlockSpec((bm, LANES), lambda i, j: (i, 0)),
            pl.BlockSpec((Q_LORA, bq), lambda i, j: (0, j)),
            pl.BlockSpec((KV_LORA, bq), lambda i, j: (0, j)),
        ],
        out_specs=[
            pl.BlockSpec((bm, bq), lambda i, j: (i, j)),
            pl.BlockSpec((bm, bq), lambda i, j: (i, j)),
            pl.BlockSpec((bm, heads * V_PAD), lambda i, j: (i, j)),
        ],
        out_shape=[
            jax.ShapeDtypeStruct((s, MLA_HEADS * QK_PAD), BF16),
            jax.ShapeDtypeStruct((s, MLA_HEADS * QK_PAD), BF16),
            jax.ShapeDtypeStruct((s, MLA_HEADS * V_PAD), BF16),
        ],
        scratch_shapes=[
            pltpu.VMEM((bm, Q_LORA), BF16),
            pltpu.VMEM((bm, KV_LORA), BF16),
            pltpu.VMEM((bm, LANES), BF16),
        ],
        compiler_params=_params("parallel", "arbitrary"),
        name="mla_prep",
    )(proj, proj, proj, gq, gkv, cos, sin, wq, wkv)


def _normalize(acc):
    return acc[:, :V_HEAD] / acc[:, V_HEAD:V_HEAD + 1]


def _mla_attn_kernel(q_ref, k_ref, v_ref, o_ref, *, tq, tk):
    for u in range(q_ref.shape[0] // tq):
        q_rows = slice(u * tq, (u + 1) * tq)
        q = q_ref[q_rows, :]
        m = acc = None
        for c in range(k_ref.shape[0] // tk):
            rows = slice(c * tk, (c + 1) * tk)
            s = lax.dot_general(q, k_ref[rows, :], _NT, preferred_element_type=F32)
            m_c = jnp.max(s, axis=-1, keepdims=True)
            if c == 0:
                m = m_c
                acc = jnp.dot(jnp.exp2(s - m).astype(BF16), v_ref[rows, :], preferred_element_type=F32)
            else:
                m_new = jnp.maximum(m, m_c)
                pv = jnp.dot(jnp.exp2(s - m_new).astype(BF16), v_ref[rows, :], preferred_element_type=F32)
                acc = jnp.exp2(m - m_new) * acc + pv
                m = m_new
        o_ref[q_rows, :] = _normalize(acc).astype(o_ref.dtype)


def _mla_attn(q, k, v, tq=512, sub=4, tk=1024):
    s = q.shape[0]
    kern = functools.partial(_mla_attn_kernel, tq=tq, tk=tk)
    return pl.pallas_call(
        kern,
        grid=(MLA_HEADS, s // (tq * sub)),
        in_specs=[
            pl.BlockSpec((tq * sub, QK_PAD), lambda h, i: (i, h)),
            pl.BlockSpec((s, QK_PAD), lambda h, i: (0, h)),
            pl.BlockSpec((s, V_PAD), lambda h, i: (0, h)),
        ],
        out_specs=pl.BlockSpec((tq * sub, V_HEAD), lambda h, i: (i, h)),
        out_shape=jax.ShapeDtypeStruct((s, MLA_WIDTH), BF16),
        compiler_params=_params("parallel", "arbitrary"),
        name="mla_attn",
    )(q, k, v)


DIL_T = 256
DIL_SUB = 16
DIL_REACH = max(w // 2 for w, _ in DIL_PATTERNS)
DIL_SIDE = DIL_REACH // DIL_T
DIL_NCHUNK = 2 * DIL_SIDE + 1


def _dil_bias(r, slope):
    row = lax.broadcasted_iota(jnp.int32, (DIL_T, DIL_T), 0)
    col = lax.broadcasted_iota(jnp.int32, (DIL_T, DIL_T), 1)
    d = (r - DIL_SIDE) * DIL_T + col - row
    ad = jnp.abs(d)
    cnt = jnp.zeros((DIL_T, DIL_T), F32)
    for window, dil in DIL_PATTERNS:
        member = ((d & (dil - 1)) == 0) & (ad <= window // 2)
        cnt = cnt + member.astype(F32)
    bias = jnp.log2(jnp.maximum(cnt, 1.0)) - (slope * LOG2E) * ad.astype(F32)
    return jnp.where(cnt > 0.0, bias, MASK_BIAS)


def _dil_attn_kernel(slope_ref, q_ref, k_ref, v_ref, o_ref, b_ref, vp_ref, *, nblk):
    h = pl.program_id(0)
    i = pl.program_id(1)

    @pl.when(i == 0)
    def _():
        slope = slope_ref[h]
        for r in range(DIL_NCHUNK):
            b_ref[r] = _dil_bias(r, slope)
        b_ref[DIL_NCHUNK] = jnp.full((DIL_T, DIL_T), MASK_BIAS, F32)
        vp_ref[:, :DIL_HEAD_DIM] = v_ref[...]
        vp_ref[:, DIL_HEAD_DIM:] = jnp.ones((vp_ref.shape[0], V_PAD - DIL_HEAD_DIM), BF16)

    for u in range(DIL_SUB):
        rows = slice(u * DIL_T, (u + 1) * DIL_T)
        q = q_ref[rows, :]
        scores, starts = [], []
        for r in range(DIL_NCHUNK):
            kc = i * DIL_SUB + u + (r - DIL_SIDE)
            inside = (kc >= 0) & (kc < nblk)
            ks = pl.multiple_of(jnp.clip(kc, 0, nblk - 1) * DIL_T, DIL_T)
            bias = b_ref[jnp.where(inside, r, DIL_NCHUNK)]
            scores.append(lax.dot_general(q, k_ref[pl.ds(ks, DIL_T), :], _NT, preferred_element_type=F32) + bias)
            starts.append(ks)
        m = functools.reduce(jnp.maximum, [jnp.max(s, axis=-1, keepdims=True) for s in scores])
        acc = None
        for s, ks in zip(scores, starts):
            pv = jnp.dot(jnp.exp2(s - m).astype(BF16), vp_ref[pl.ds(ks, DIL_T), :], preferred_element_type=F32)
            acc = pv if acc is None else acc + pv
        o_ref[rows, :] = _normalize(acc).astype(o_ref.dtype)


def _dil_attn(proj, slopes):
    s = proj.shape[0]
    nblk = s // DIL_T
    kern = functools.partial(_dil_attn_kernel, nblk=nblk)
    hd = DIL_HEAD_DIM
    return pl.pallas_call(
        kern,
        grid=(DIL_HEADS, nblk // DIL_SUB),
        in_specs=[
            pl.BlockSpec(memory_space=pltpu.SMEM),
            pl.BlockSpec((DIL_T * DIL_SUB, hd), lambda h, i: (i, COL_QB // hd + h)),
            pl.BlockSpec((s, hd), lambda h, i: (0, COL_KB // hd + h)),
            pl.BlockSpec((s, hd), lambda h, i: (0, COL_VB // hd + h)),
        ],
        out_specs=pl.BlockSpec((DIL_T * DIL_SUB, hd), lambda h, i: (i, h)),
        out_shape=jax.ShapeDtypeStruct((s, DIL_WIDTH), BF16),
        scratch_shapes=[
            pltpu.VMEM((DIL_NCHUNK + 1, DIL_T, DIL_T), F32),
            pltpu.VMEM((s, V_PAD), BF16),
        ],
        compiler_params=_params("arbitrary", "arbitrary"),
        name="dil_attn",
    )(slopes, proj, proj, proj)


def _gated_norm(a_ref, gate_ref, g_ref, y_ref):
    for r in range(0, a_ref.shape[0], ROW_CHUNK):
        rows = slice(r, r + ROW_CHUNK)
        gate = gate_ref[rows, :].astype(F32)
        y = _rms(a_ref[rows, :].astype(F32), g_ref[...]) * (gate * jax.nn.sigmoid(gate))
        y_ref[rows, :] = y.astype(y_ref.dtype)


def _tail_kernel(aa_ref, ab_ref, ga_ref, gb_ref, gma_ref, gmb_ref, x_ref, w_ref, gple_ref, p_ref, wp_ref,
                 gfin_ref, o_ref, yh_ref, *, nj, bn, last_layer):
    jj = pl.program_id(1)

    @pl.when(jj == 0)
    def _():
        _gated_norm(aa_ref, ga_ref, gma_ref, yh_ref.at[:, :MLA_WIDTH])
        _gated_norm(ab_ref, gb_ref, gmb_ref, yh_ref.at[:, MLA_WIDTH:])

    @pl.when(jj < nj)
    def _():
        cols = pl.ds(pl.multiple_of(jj * bn, bn), bn)
        o_ref[:, cols] = x_ref[...] + jnp.dot(yh_ref[...], w_ref[...], preferred_element_type=F32)

    @pl.when(jj == nj)
    def _():
        _rms_rows(o_ref, gple_ref, yh_ref)

    @pl.when(jj >= nj)
    def _():
        cols = pl.ds(pl.multiple_of((jj - nj) * bn, bn), bn)
        gate = jax.nn.sigmoid(jnp.dot(yh_ref[...], w_ref[...], preferred_element_type=F32))
        ple = jnp.dot(p_ref[...].astype(BF16), wp_ref[...], preferred_element_type=F32)
        o_ref[:, cols] = o_ref[:, cols] + ple * gate

    if last_layer:
        @pl.when(jj == 2 * nj - 1)
        def _():
            _rms_rows(o_ref, gfin_ref, o_ref)


def _tail(attn_a, attn_b, proj, g_a, g_b, x, w_cat, g_ple, p, w_ple, g_fin, last_layer, bm=512, bn=512):
    s, d = x.shape
    assert d == MLA_WIDTH + DIL_WIDTH
    c = p.shape[1]
    nj = d // bn
    kern = functools.partial(_tail_kernel, nj=nj, bn=bn, last_layer=last_layer)
    first = lambda i, jj: (i, jnp.minimum(jj, nj - 1))
    second = lambda i, jj: (0, jnp.maximum(jj - nj, 0))
    row = lambda i, jj: (i, 0)
    vec = lambda i, jj: (0, 0)
    return pl.pallas_call(
        kern,
        grid=(s // bm, 2 * nj),
        in_specs=[
            pl.BlockSpec((bm, MLA_WIDTH), row),
            pl.BlockSpec((bm, DIL_WIDTH), row),
            pl.BlockSpec((bm, MLA_WIDTH), lambda i, jj: (i, COL_GATE_A // MLA_WIDTH)),
            pl.BlockSpec((bm, DIL_WIDTH), lambda i, jj: (i, COL_GATE_B // DIL_WIDTH)),
            pl.BlockSpec((1, MLA_WIDTH), vec),
            pl.BlockSpec((1, DIL_WIDTH), vec),
            pl.BlockSpec((bm, bn), first),
            pl.BlockSpec((d, bn), lambda i, jj: (jj // nj, jj % nj)),
            pl.BlockSpec((1, d), vec),
            pl.BlockSpec((bm, c), row),
            pl.BlockSpec((c, bn), second),
            pl.BlockSpec((1, d), vec),
        ],
        out_specs=pl.BlockSpec((bm, d), row),
        out_shape=jax.ShapeDtypeStruct((s, d), F32),
        scratch_shapes=[pltpu.VMEM((bm, d), BF16)],
        compiler_params=_params("parallel", "arbitrary"),
        name="tail",
    )(attn_a, attn_b, proj, proj, g_a, g_b, x, w_cat, g_ple, p, w_ple, g_fin)


def _stack_cast_kernel(a_ref, b_ref, o_ref, *, nb):
    t = pl.program_id(0)

    @pl.when(t < nb)
    def _():
        o_ref[...] = a_ref[...].astype(o_ref.dtype)

    @pl.when(t >= nb)
    def _():
        o_ref[...] = b_ref[...].astype(o_ref.dtype)


def _stack_cast(a, b, rb=512):
    r, c = a.shape
    assert a.shape == b.shape and r % rb == 0
    nb = r // rb
    return pl.pallas_call(
        functools.partial(_stack_cast_kernel, nb=nb),
        grid=(2 * nb,),
        in_specs=[
            pl.BlockSpec((rb, c), lambda t: (jnp.minimum(t, nb - 1), 0)),
            pl.BlockSpec((rb, c), lambda t: (jnp.maximum(t - nb, 0), 0)),
        ],
        out_specs=pl.BlockSpec((rb, c), lambda t: (t, 0)),
        out_shape=jax.ShapeDtypeStruct((2 * r, c), BF16),
        compiler_params=_params("arbitrary"),
        name="stack_cast",
    )(a, b)


def _prep_w_in(w):
    wt = w.T
    d = wt.shape[1]
    lat = Q_LORA + KV_LORA
    head = lat + QK_ROPE
    row_scale = np.ones((wt.shape[0] - head, 1), np.float32)
    row_scale[COL_QB:COL_KB] = DIL_HEAD_DIM ** -0.5 * LOG2E
    w_main = (wt[head:] * row_scale).astype(BF16)
    small = lax.optimization_barrier(wt[:head])
    w_tail = jnp.pad(small, ((0, PROJ_WIDTH - COL_KR - QK_ROPE), (0, 0))).astype(BF16)
    return w_main, w_tail


def _prep_w_uq(w):
    r = w.shape[0]
    w = w.reshape(r, MLA_HEADS, QK_NOPE + QK_ROPE) * ((QK_NOPE + QK_ROPE) ** -0.5 * LOG2E)
    w = jnp.pad(w, ((0, 0), (0, 0), (0, QK_PAD - QK_NOPE - QK_ROPE)))
    return w.reshape(r, MLA_HEADS * QK_PAD).astype(BF16)


def _rope_tables(positions):
    inv = ROPE_THETA ** (-jnp.arange(ROPE_HALF, dtype=F32) / ROPE_HALF)
    ang = positions.astype(F32)[:, None] * inv
    cos, sin = jnp.cos(ang), jnp.sin(ang)
    zero = jnp.zeros_like(cos)
    return (jnp.concatenate([cos, cos, zero, zero], axis=1),
            jnp.concatenate([-sin, sin, zero, zero], axis=1))


def kernel(x, p, positions, g_mix, w_in, g_q_latent, w_uq, g_kv_latent, w_ukv, g_out_mla, g_out_dil,
           w_out, w_ple, g_ple, w_ple_gate, g_final):
    b, s, d = x.shape
    assert b == 1 and s % (DIL_T * DIL_SUB) == 0
    slopes = jnp.asarray(2.0 ** (-ALIBI_MAX_BIAS * np.arange(1, DIL_HEADS + 1) / DIL_HEADS), F32)
    x2d = x.reshape(s, d)
    cos, sin = _rope_tables(positions[0])
    depth = g_mix.shape[0]
    for i in range(depth):
        proj = _in_proj(x2d, g_mix[i][None], *_prep_w_in(w_in[i]))
        q_a, k_a, v_a = _mla_prep(proj, g_q_latent[i][None], g_kv_latent[i][None], cos, sin,
                                  _prep_w_uq(w_uq[i]), w_ukv[i].astype(BF16))
        attn_a = _mla_attn(q_a, k_a, v_a)
        attn_b = _dil_attn(proj, slopes)
        w_cat = _stack_cast(w_out[i], w_ple_gate[i])
        x2d = _tail(attn_a, attn_b, proj, g_out_mla[i][None], g_out_dil[i][None], x2d, w_cat,
                    g_ple[i][None], p[i, 0], w_ple[i].astype(BF16), g_final[None], i == depth - 1)
    return x2d.reshape(b, s, d)
```

```python
import functools

import numpy as np
import jax
import jax.numpy as jnp
from jax import lax
from jax.experimental import pallas as pl
from jax.experimental.pallas import tpu as pltpu

F32 = jnp.float32
BF16 = jnp.bfloat16

EPS = 1e-6
ROPE_THETA = 10000.0
ALIBI_MAX_BIAS = 8.0

MLA_HEADS = 16
Q_LORA = 1024
KV_LORA = 512
QK_NOPE = 128
QK_ROPE = 64
V_HEAD = 128
MLA_WIDTH = MLA_HEADS * V_HEAD
DIL_HEADS = 16
DIL_HEAD_DIM = 128
DIL_WIDTH = DIL_HEADS * DIL_HEAD_DIM
DIL_PATTERNS = ((128, 1), (512, 4), (2048, 16))

LANES = 128
QK_PAD = 256
V_PAD = 2 * LANES
ROPE_HALF = QK_ROPE // 2
LOG2E = float(np.log2(np.e))

COL_GATE_A = 0
COL_QB = COL_GATE_A + MLA_WIDTH
COL_KB = COL_QB + DIL_WIDTH
COL_VB = COL_KB + DIL_WIDTH
COL_GATE_B = COL_VB + DIL_WIDTH
COL_CQ = COL_GATE_B + DIL_WIDTH
COL_CKV = COL_CQ + Q_LORA
COL_KR = COL_CKV + KV_LORA
PROJ_WIDTH = 12288

MASK_BIAS = -1e30

ROW_CHUNK = 64

VMEM_LIMIT = 60 * 1024 * 1024


def _params(*sem):
    return pltpu.CompilerParams(dimension_semantics=sem, vmem_limit_bytes=VMEM_LIMIT)


def _rms(x, g):
    return x * lax.rsqrt(jnp.mean(x * x, axis=-1, keepdims=True) + EPS) * g


def _rms_rows(x_ref, g_ref, h_ref):
    for r in range(0, x_ref.shape[0], ROW_CHUNK):
        rows = slice(r, r + ROW_CHUNK)
        h_ref[rows, :] = _rms(x_ref[rows, :].astype(F32), g_ref[...]).astype(h_ref.dtype)


_NT = (((1,), (1,)), ((), ()))


def _in_proj_kernel(x_ref, g_ref, wm_ref, wt_ref, o_ref, h_ref, *, parts, n_main):
    j = pl.program_id(1)
    bx = x_ref.shape[0]

    for part in range(parts):
        @pl.when(j == part)
        def _():
            _rms_rows(x_ref, g_ref, h_ref.at[part * bx:(part + 1) * bx, :])

    @pl.when((j >= parts) & (j < parts + n_main))
    def _():
        o_ref[...] = lax.dot_general(h_ref[...], wm_ref[...], _NT, preferred_element_type=F32).astype(o_ref.dtype)

    @pl.when(j >= parts + n_main)
    def _():
        o_ref[...] = lax.dot_general(h_ref[...], wt_ref[...], _NT, preferred_element_type=F32).astype(o_ref.dtype)


def _in_proj(x, g, w_main, w_tail, bx=512, parts=4, bn=512):
    s, d = x.shape
    bm = bx * parts
    n_main = w_main.shape[0] // bn
    n_tail = w_tail.shape[0] // bn
    kern = functools.partial(_in_proj_kernel, parts=parts, n_main=n_main)
    return pl.pallas_call(
        kern,
        grid=(s // bm, parts + n_main + n_tail),
        in_specs=[
            pl.BlockSpec((bx, d), lambda i, j: (i * parts + jnp.minimum(j, parts - 1), 0)),
            pl.BlockSpec((1, d), lambda i, j: (0, 0)),
            pl.BlockSpec((bn, d), lambda i, j: (jnp.clip(j - parts, 0, n_main - 1), 0)),
            pl.BlockSpec((bn, d), lambda i, j: (jnp.clip(j - parts - n_main, 0, n_tail - 1), 0)),
        ],
        out_specs=pl.BlockSpec((bm, bn), lambda i, j: (i, jnp.maximum(j - parts, 0))),
        out_shape=jax.ShapeDtypeStruct((s, (n_main + n_tail) * bn), BF16),
        scratch_shapes=[pltpu.VMEM((bm, d), BF16)],
        compiler_params=_params("parallel", "arbitrary"),
        name="in_proj",
    )(x, g, w_main, w_tail)


def _rope(t, cos, sin):
    lane = lax.broadcasted_iota(jnp.int32, t.shape, 1)
    swapped = jnp.where(lane < ROPE_HALF, pltpu.roll(t, LANES - ROPE_HALF, axis=1), pltpu.roll(t, ROPE_HALF, axis=1))
    return t * cos + swapped * sin


def _mla_prep_kernel(cq_ref, ckv_ref, kr_ref, gq_ref, gkv_ref, cos_ref, sin_ref, wq_ref, wkv_ref,
                     q_out, k_out, v_out, cqn_ref, ckvn_ref, kpe_ref, *, heads):
    @pl.when(pl.program_id(1) == 0)
    def _():
        cqn_ref[...] = _rms(cq_ref[...].astype(F32), gq_ref[...]).astype(BF16)
        ckvn_ref[...] = _rms(ckv_ref[...].astype(F32), gkv_ref[...]).astype(BF16)
        kpe_ref[...] = _rope(kr_ref[...].astype(F32), cos_ref[...], sin_ref[...]).astype(BF16)

    q = jnp.dot(cqn_ref[...], wq_ref[...], preferred_element_type=F32)
    kv = jnp.dot(ckvn_ref[...], wkv_ref[...], preferred_element_type=F32)
    cos = cos_ref[...]
    sin = sin_ref[...]
    for g in range(heads):
        lo = g * QK_PAD
        mid = lo + LANES
        hi = lo + QK_PAD
        q_out[:, lo:mid] = q[:, lo:mid].astype(BF16)
        q_out[:, mid:hi] = _rope(q[:, mid:hi], cos, sin).astype(BF16)
        k_out[:, lo:mid] = kv[:, lo:mid].astype(BF16)
        k_out[:, mid:hi] = kpe_ref[...]
        v_out[:, lo:mid] = kv[:, mid:hi].astype(BF16)
        v_out[:, mid:hi] = jnp.ones((v_out.shape[0], LANES), BF16)


def _mla_prep(proj, gq, gkv, cos, sin, wq, wkv, bm=256, heads=MLA_HEADS):
    s = proj.shape[0]
    bq = heads * QK_PAD
    kern = functools.partial(_mla_prep_kernel, heads=heads)
    return pl.pallas_call(
        kern,
        grid=(s // bm, MLA_HEADS // heads),
        in_specs=[
            pl.BlockSpec((bm, Q_LORA), lambda i, j: (i, COL_CQ // Q_LORA)),
            pl.BlockSpec((bm, KV_LORA), lambda i, j: (i, COL_CKV // KV_LORA)),
            pl.BlockSpec((bm, LANES), lambda i, j: (i, COL_KR // LANES)),
            pl.BlockSpec((1, Q_LORA), lambda i, j: (0, 0)),
            pl.BlockSpec((1, KV_LORA), lambda i, j: (0, 0)),
            pl.BlockSpec((bm, LANES), lambda i, j: (i, 0)),
            pl.BlockSpec((bm, LANES), lambda i, j: (i, 0)),
            pl.BlockSpec((Q_LORA, bq), lambda i, j: (0, j)),
            pl.BlockSpec((KV_LORA, bq), lambda i, j: (0, j)),
        ],
        out_specs=[
            pl.BlockSpec((bm, bq), lambda i, j: (i, j)),
            pl.BlockSpec((bm, bq), lambda i, j: (i, j)),
            pl.BlockSpec((bm, heads * V_PAD), lambda i, j: (i, j)),
        ],
        out_shape=[
            jax.ShapeDtypeStruct((s, MLA_HEADS * QK_PAD), BF16),
            jax.ShapeDtypeStruct((s, MLA_HEADS * QK_PAD), BF16),
            jax.ShapeDtypeStruct((s, MLA_HEADS * V_PAD), BF16),
        ],
        scratch_shapes=[
            pltpu.VMEM((bm, Q_LORA), BF16),
            pltpu.VMEM((bm, KV_LORA), BF16),
            pltpu.VMEM((bm, LANES), BF16),
        ],
        compiler_params=_params("parallel", "arbitrary"),
        name="mla_prep",
    )(proj, proj, proj, gq, gkv, cos, sin, wq, wkv)


def _normalize(acc):
    return acc[:, :V_HEAD] / acc[:, V_HEAD:V_HEAD + 1]


def _mla_attn_kernel(q_ref, k_ref, v_ref, wa_ref, wb_ref, o_ref, wc_ref, *, tq, tk, nb):
    t = pl.program_id(0) * pl.num_programs(1) + pl.program_id(1)

    @pl.when(t < nb)
    def _():
        wc_ref[...] = wa_ref[...].astype(wc_ref.dtype)

    @pl.when(t >= nb)
    def _():
        wc_ref[...] = wb_ref[...].astype(wc_ref.dtype)

    for u in range(q_ref.shape[0] // tq):
        q_rows = slice(u * tq, (u + 1) * tq)
        q = q_ref[q_rows, :]
        m = acc = None
        for c in range(k_ref.shape[0] // tk):
            rows = slice(c * tk, (c + 1) * tk)
            s = lax.dot_general(q, k_ref[rows, :], _NT, preferred_element_type=F32)
            m_c = jnp.max(s, axis=-1, keepdims=True)
            if c == 0:
                m = m_c
                acc = jnp.dot(jnp.exp2(s - m).astype(BF16), v_ref[rows, :], preferred_element_type=F32)
            else:
                m_new = jnp.maximum(m, m_c)
                pv = jnp.dot(jnp.exp2(s - m_new).astype(BF16), v_ref[rows, :], preferred_element_type=F32)
                acc = jnp.exp2(m - m_new) * acc + pv
                m = m_new
        o_ref[q_rows, :] = _normalize(acc).astype(o_ref.dtype)


def _mla_attn(q, k, v, wa, wb, tq=512, sub=4, tk=1024):
    s = q.shape[0]
    ni = s // (tq * sub)
    steps = MLA_HEADS * ni
    r, c = wa.shape
    assert wa.shape == wb.shape and (2 * r) % steps == 0
    rb = 2 * r // steps
    nb = r // rb
    kern = functools.partial(_mla_attn_kernel, tq=tq, tk=tk, nb=nb)
    return pl.pallas_call(
        kern,
        grid=(MLA_HEADS, ni),
        in_specs=[
            pl.BlockSpec((tq * sub, QK_PAD), lambda h, i: (i, h)),
            pl.BlockSpec((s, QK_PAD), lambda h, i: (0, h)),
            pl.BlockSpec((s, V_PAD), lambda h, i: (0, h)),
            pl.BlockSpec((rb, c), lambda h, i: (jnp.minimum(h * ni + i, nb - 1), 0)),
            pl.BlockSpec((rb, c), lambda h, i: (jnp.maximum(h * ni + i - nb, 0), 0)),
        ],
        out_specs=[
            pl.BlockSpec((tq * sub, V_HEAD), lambda h, i: (i, h)),
            pl.BlockSpec((rb, c), lambda h, i: (h * ni + i, 0)),
        ],
        out_shape=[
            jax.ShapeDtypeStruct((s, MLA_WIDTH), BF16),
            jax.ShapeDtypeStruct((2 * r, c), BF16),
        ],
        compiler_params=_params("arbitrary", "arbitrary"),
        name="mla_attn",
    )(q, k, v, wa, wb)


DIL_T = 256
DIL_SUB = 16
DIL_REACH = max(w // 2 for w, _ in DIL_PATTERNS)
DIL_SIDE = DIL_REACH // DIL_T
DIL_NCHUNK = 2 * DIL_SIDE + 1


def _dil_bias(r, slope):
    row = lax.broadcasted_iota(jnp.int32, (DIL_T, DIL_T), 0)
    col = lax.broadcasted_iota(jnp.int32, (DIL_T, DIL_T), 1)
    d = (r - DIL_SIDE) * DIL_T + col - row
    ad = jnp.abs(d)
    cnt = jnp.zeros((DIL_T, DIL_T), F32)
    for window, dil in DIL_PATTERNS:
        member = ((d & (dil - 1)) == 0) & (ad <= window // 2)
        cnt = cnt + member.astype(F32)
    bias = jnp.log2(jnp.maximum(cnt, 1.0)) - (slope * LOG2E) * ad.astype(F32)
    return jnp.where(cnt > 0.0, bias, MASK_BIAS)


def _dil_attn_kernel(slope_ref, q_ref, k_ref, v_ref, o_ref, b_ref, vp_ref, *, nblk):
    h = pl.program_id(0)
    i = pl.program_id(1)

    @pl.when(i == 0)
    def _():
        slope = slope_ref[h]
        for r in range(DIL_NCHUNK):
            b_ref[r] = _dil_bias(r, slope)
        b_ref[DIL_NCHUNK] = jnp.full((DIL_T, DIL_T), MASK_BIAS, F32)
        vp_ref[:, :DIL_HEAD_DIM] = v_ref[...]
        vp_ref[:, DIL_HEAD_DIM:] = jnp.ones((vp_ref.shape[0], V_PAD - DIL_HEAD_DIM), BF16)

    for u in range(DIL_SUB):
        rows = slice(u * DIL_T, (u + 1) * DIL_T)
        q = q_ref[rows, :]
        scores, starts = [], []
        for r in range(DIL_NCHUNK):
            kc = i * DIL_SUB + u + (r - DIL_SIDE)
            inside = (kc >= 0) & (kc < nblk)
            ks = pl.multiple_of(jnp.clip(kc, 0, nblk - 1) * DIL_T, DIL_T)
            bias = b_ref[jnp.where(inside, r, DIL_NCHUNK)]
            scores.append(lax.dot_general(q, k_ref[pl.ds(ks, DIL_T), :], _NT, preferred_element_type=F32) + bias)
            starts.append(ks)
        m = functools.reduce(jnp.maximum, [jnp.max(s, axis=-1, keepdims=True) for s in scores])
        acc = None
        for s, ks in zip(scores, starts):
            pv = jnp.dot(jnp.exp2(s - m).astype(BF16), vp_ref[pl.ds(ks, DIL_T), :], preferred_element_type=F32)
            acc = pv if acc is None else acc + pv
        o_ref[rows, :] = _normalize(acc).astype(o_ref.dtype)


def _dil_attn(proj, slopes):
    s = proj.shape[0]
    nblk = s // DIL_T
    kern = functools.partial(_dil_attn_kernel, nblk=nblk)
    hd = DIL_HEAD_DIM
    return pl.pallas_call(
        kern,
        grid=(DIL_HEADS, nblk // DIL_SUB),
        in_specs=[
            pl.BlockSpec(memory_space=pltpu.SMEM),
            pl.BlockSpec((DIL_T * DIL_SUB, hd), lambda h, i: (i, COL_QB // hd + h)),
            pl.BlockSpec((s, hd), lambda h, i: (0, COL_KB // hd + h)),
            pl.BlockSpec((s, hd), lambda h, i: (0, COL_VB // hd + h)),
        ],
        out_specs=pl.BlockSpec((DIL_T * DIL_SUB, hd), lambda h, i: (i, h)),
        out_shape=jax.ShapeDtypeStruct((s, DIL_WIDTH), BF16),
        scratch_shapes=[
            pltpu.VMEM((DIL_NCHUNK + 1, DIL_T, DIL_T), F32),
            pltpu.VMEM((s, V_PAD), BF16),
        ],
        compiler_params=_params("arbitrary", "arbitrary"),
        name="dil_attn",
    )(slopes, proj, proj, proj)


def _gated_norm(a_ref, gate_ref, g_ref, y_ref):
    for r in range(0, a_ref.shape[0], ROW_CHUNK):
        rows = slice(r, r + ROW_CHUNK)
        gate = gate_ref[rows, :].astype(F32)
        y = _rms(a_ref[rows, :].astype(F32), g_ref[...]) * (gate * jax.nn.sigmoid(gate))
        y_ref[rows, :] = y.astype(y_ref.dtype)


def _tail_kernel(aa_ref, ab_ref, ga_ref, gb_ref, gma_ref, gmb_ref, x_ref, w_ref, gple_ref, p_ref, wp_ref,
                 gfin_ref, o_ref, yh_ref, *, nj, bn, last_layer):
    jj = pl.program_id(1)

    @pl.when(jj == 0)
    def _():
        _gated_norm(aa_ref, ga_ref, gma_ref, yh_ref.at[:, :MLA_WIDTH])
        _gated_norm(ab_ref, gb_ref, gmb_ref, yh_ref.at[:, MLA_WIDTH:])

    @pl.when(jj < nj)
    def _():
        cols = pl.ds(pl.multiple_of(jj * bn, bn), bn)
        o_ref[:, cols] = x_ref[...] + jnp.dot(yh_ref[...], w_ref[...], preferred_element_type=F32)

    @pl.when(jj == nj)
    def _():
        _rms_rows(o_ref, gple_ref, yh_ref)

    @pl.when(jj >= nj)
    def _():
        cols = pl.ds(pl.multiple_of((jj - nj) * bn, bn), bn)
        gate = jax.nn.sigmoid(jnp.dot(yh_ref[...], w_ref[...], preferred_element_type=F32))
        ple = jnp.dot(p_ref[...].astype(BF16), wp_ref[...], preferred_element_type=F32)
        o_ref[:, cols] = o_ref[:, cols] + ple * gate

    if last_layer:
        @pl.when(jj == 2 * nj - 1)
        def _():
            _rms_rows(o_ref, gfin_ref, o_ref)


def _tail(attn_a, attn_b, proj, g_a, g_b, x, w_cat, g_ple, p, w_ple, g_fin, last_layer, bm=512, bn=512):
    s, d = x.shape
    assert d == MLA_WIDTH + DIL_WIDTH
    c = p.shape[1]
    nj = d // bn
    kern = functools.partial(_tail_kernel, nj=nj, bn=bn, last_layer=last_layer)
    first = lambda i, jj: (i, jnp.minimum(jj, nj - 1))
    second = lambda i, jj: (0, jnp.maximum(jj - nj, 0))
    row = lambda i, jj: (i, 0)
    vec = lambda i, jj: (0, 0)
    return pl.pallas_call(
        kern,
        grid=(s // bm, 2 * nj),
        in_specs=[
            pl.BlockSpec((bm, MLA_WIDTH), row),
            pl.BlockSpec((bm, DIL_WIDTH), row),
            pl.BlockSpec((bm, MLA_WIDTH), lambda i, jj: (i, COL_GATE_A // MLA_WIDTH)),
            pl.BlockSpec((bm, DIL_WIDTH), lambda i, jj: (i, COL_GATE_B // DIL_WIDTH)),
            pl.BlockSpec((1, MLA_WIDTH), vec),
            pl.BlockSpec((1, DIL_WIDTH), vec),
            pl.BlockSpec((bm, bn), first),
            pl.BlockSpec((d, bn), lambda i, jj: (jj // nj, jj % nj)),
            pl.BlockSpec((1, d), vec),
            pl.BlockSpec((bm, c), row),
            pl.BlockSpec((c, bn), second),
            pl.BlockSpec((1, d), vec),
        ],
        out_specs=pl.BlockSpec((bm, d), row),
        out_shape=jax.ShapeDtypeStruct((s, d), F32),
        scratch_shapes=[pltpu.VMEM((bm, d), BF16)],
        compiler_params=_params("parallel", "arbitrary"),
        name="tail",
    )(attn_a, attn_b, proj, proj, g_a, g_b, x, w_cat, g_ple, p, w_ple, g_fin)


def _prep_w_in(w):
    wt = w.T
    d = wt.shape[1]
    lat = Q_LORA + KV_LORA
    head = lat + QK_ROPE
    row_scale = np.ones((wt.shape[0] - head, 1), np.float32)
    row_scale[COL_QB:COL_KB] = DIL_HEAD_DIM ** -0.5 * LOG2E
    w_main = (wt[head:] * row_scale).astype(BF16)
    small = lax.optimization_barrier(wt[:head])
    w_tail = jnp.pad(small, ((0, PROJ_WIDTH - COL_KR - QK_ROPE), (0, 0))).astype(BF16)
    return w_main, w_tail


def _prep_w_uq(w):
    r = w.shape[0]
    w = w.reshape(r, MLA_HEADS, QK_NOPE + QK_ROPE) * ((QK_NOPE + QK_ROPE) ** -0.5 * LOG2E)
    w = jnp.pad(w, ((0, 0), (0, 0), (0, QK_PAD - QK_NOPE - QK_ROPE)))
    return w.reshape(r, MLA_HEADS * QK_PAD).astype(BF16)


def _rope_tables(positions):
    inv = ROPE_THETA ** (-jnp.arange(ROPE_HALF, dtype=F32) / ROPE_HALF)
    ang = positions.astype(F32)[:, None] * inv
    cos, sin = jnp.cos(ang), jnp.sin(ang)
    zero = jnp.zeros_like(cos)
    return (jnp.concatenate([cos, cos, zero, zero], axis=1),
            jnp.concatenate([-sin, sin, zero, zero], axis=1))


def kernel(x, p, positions, g_mix, w_in, g_q_latent, w_uq, g_kv_latent, w_ukv, g_out_mla, g_out_dil,
           w_out, w_ple, g_ple, w_ple_gate, g_final):
    b, s, d = x.shape
    assert b == 1 and s % (DIL_T * DIL_SUB) == 0
    slopes = jnp.asarray(2.0 ** (-ALIBI_MAX_BIAS * np.arange(1, DIL_HEADS + 1) / DIL_HEADS), F32)
    x2d = x.reshape(s, d)
    cos, sin = _rope_tables(positions[0])
    depth = g_mix.shape[0]
    for i in range(depth):
        proj = _in_proj(x2d, g_mix[i][None], *_prep_w_in(w_in[i]))
        q_a, k_a, v_a = _mla_prep(proj, g_q_latent[i][None], g_kv_latent[i][None], cos, sin,
                                  _prep_w_uq(w_uq[i]), w_ukv[i].astype(BF16))
        attn_a, w_cat = _mla_attn(q_a, k_a, v_a, w_out[i], w_ple_gate[i])
        attn_b = _dil_attn(proj, slopes)
        x2d = _tail(attn_a, attn_b, proj, g_out_mla[i][None], g_out_dil[i][None], x2d, w_cat,
                    g_ple[i][None], p[i, 0], w_ple[i].astype(BF16), g_final[None], i == depth - 1)
    return x2d.reshape(b, s, d)
```

```python
import functools

import numpy as np
import jax
import jax.numpy as jnp
from jax import lax
from jax.experimental import pallas as pl
from jax.experimental.pallas import tpu as pltpu

F32 = jnp.float32
BF16 = jnp.bfloat16

EPS = 1e-6
ROPE_THETA = 10000.0
ALIBI_MAX_BIAS = 8.0

MLA_HEADS = 16
Q_LORA = 1024
KV_LORA = 512
QK_NOPE = 128
QK_ROPE = 64
V_HEAD = 128
MLA_WIDTH = MLA_HEADS * V_HEAD
DIL_HEADS = 16
DIL_HEAD_DIM = 128
DIL_WIDTH = DIL_HEADS * DIL_HEAD_DIM
DIL_PATTERNS = ((128, 1), (512, 4), (2048, 16))

LANES = 128
QK_PAD = 256
V_PAD = 2 * LANES
ROPE_HALF = QK_ROPE // 2
LOG2E = float(np.log2(np.e))

COL_GATE_A = 0
COL_QB = COL_GATE_A + MLA_WIDTH
COL_KB = COL_QB + DIL_WIDTH
COL_VB = COL_KB + DIL_WIDTH
COL_GATE_B = COL_VB + DIL_WIDTH
COL_CQ = COL_GATE_B + DIL_WIDTH
COL_CKV = COL_CQ + Q_LORA
COL_KR = COL_CKV + KV_LORA
PROJ_WIDTH = 12288

MASK_BIAS = -1e30

ROW_CHUNK = 64

VMEM_LIMIT = 60 * 1024 * 1024


def _params(*sem):
    return pltpu.CompilerParams(dimension_semantics=sem, vmem_limit_bytes=VMEM_LIMIT)


def _rms(x, g):
    return x * lax.rsqrt(jnp.mean(x * x, axis=-1, keepdims=True) + EPS) * g


def _rms_rows(x_ref, g_ref, h_ref):
    for r in range(0, x_ref.shape[0], ROW_CHUNK):
        rows = slice(r, r + ROW_CHUNK)
        h_ref[rows, :] = _rms(x_ref[rows, :].astype(F32), g_ref[...]).astype(h_ref.dtype)


_NT = (((1,), (1,)), ((), ()))


def _in_proj_kernel(x_ref, g_ref, w_ref, cs_ref, o_ref, h_ref, *, parts):
    j = pl.program_id(1)
    bx = x_ref.shape[0]

    for part in range(parts):
        @pl.when(j == part)
        def _():
            _rms_rows(x_ref, g_ref, h_ref.at[part * bx:(part + 1) * bx, :])

    @pl.when(j >= parts)
    def _():
        acc = lax.dot_general(h_ref[...], w_ref[...].astype(BF16), _NT, preferred_element_type=F32)
        o_ref[...] = (acc * cs_ref[...]).astype(o_ref.dtype)


def _in_proj(x, g, wt, col_scale, bx=512, parts=4, bn=512):
    s, d = x.shape
    bm = bx * parts
    feat_main = Q_LORA + KV_LORA + QK_ROPE
    n_main = (wt.shape[0] - feat_main) // bn
    n_out = PROJ_WIDTH // bn
    assert n_main * bn == COL_CQ and feat_main % 64 == 0 and (n_out - n_main) * bn <= wt.shape[0]

    def w_index(i, j):
        n = jnp.maximum(j - parts, 0)
        row = jnp.where(n < n_main, feat_main + n * bn, (n - n_main) * bn)
        return pl.multiple_of(row, 64), 0

    return pl.pallas_call(
        functools.partial(_in_proj_kernel, parts=parts),
        grid=(s // bm, parts + n_out),
        in_specs=[
            pl.BlockSpec((bx, d), lambda i, j: (i * parts + jnp.minimum(j, parts - 1), 0)),
            pl.BlockSpec((1, d), lambda i, j: (0, 0)),
            pl.BlockSpec((pl.Element(bn), pl.Element(d)), w_index),
            pl.BlockSpec((1, bn), lambda i, j: (0, jnp.maximum(j - parts, 0))),
        ],
        out_specs=pl.BlockSpec((bm, bn), lambda i, j: (i, jnp.maximum(j - parts, 0))),
        out_shape=jax.ShapeDtypeStruct((s, PROJ_WIDTH), BF16),
        scratch_shapes=[pltpu.VMEM((bm, d), BF16)],
        compiler_params=_params("parallel", "arbitrary"),
        name="in_proj",
    )(x, g, wt, col_scale)


def _rope(t, cos, sin):
    lane = lax.broadcasted_iota(jnp.int32, t.shape, 1)
    swapped = jnp.where(lane < ROPE_HALF, pltpu.roll(t, LANES - ROPE_HALF, axis=1), pltpu.roll(t, ROPE_HALF, axis=1))
    return t * cos + swapped * sin


def _mla_prep_kernel(cq_ref, ckv_ref, kr_ref, gq_ref, gkv_ref, cos_ref, sin_ref, wq_ref, wkv_ref,
                     q_out, k_out, v_out, cqn_ref, ckvn_ref, kpe_ref, *, heads):
    @pl.when(pl.program_id(1) == 0)
    def _():
        cqn_ref[...] = _rms(cq_ref[...].astype(F32), gq_ref[...]).astype(BF16)
        ckvn_ref[...] = _rms(ckv_ref[...].astype(F32), gkv_ref[...]).astype(BF16)
        kpe_ref[...] = _rope(kr_ref[...].astype(F32), cos_ref[...], sin_ref[...]).astype(BF16)

    q = jnp.dot(cqn_ref[...], wq_ref[...], preferred_element_type=F32)
    kv = jnp.dot(ckvn_ref[...], wkv_ref[...], preferred_element_type=F32)
    cos = cos_ref[...]
    sin = sin_ref[...]
    for g in range(heads):
        lo = g * QK_PAD
        mid = lo + LANES
        hi = lo + QK_PAD
        q_out[:, lo:mid] = q[:, lo:mid].astype(BF16)
        q_out[:, mid:hi] = _rope(q[:, mid:hi], cos, sin).astype(BF16)
        k_out[:, lo:mid] = kv[:, lo:mid].astype(BF16)
        k_out[:, mid:hi] = kpe_ref[...]
        v_out[:, lo:mid] = kv[:, mid:hi].astype(BF16)
        v_out[:, mid:hi] = jnp.ones((v_out.shape[0], LANES), BF16)


def _mla_prep(proj, gq, gkv, cos, sin, wq, wkv, bm=256, heads=MLA_HEADS):
    s = proj.shape[0]
    bq = heads * QK_PAD
    kern = functools.partial(_mla_prep_kernel, heads=heads)
    return pl.pallas_call(
        kern,
        grid=(s // bm, MLA_HEADS // heads),
        in_specs=[
            pl.BlockSpec((bm, Q_LORA), lambda i, j: (i, COL_CQ // Q_LORA)),
            pl.BlockSpec((bm, KV_LORA), lambda i, j: (i, COL_CKV // KV_LORA)),
            pl.BlockSpec((bm, LANES), lambda i, j: (i, COL_KR // LANES)),
            pl.BlockSpec((1, Q_LORA), lambda i, j: (0, 0)),
            pl.BlockSpec((1, KV_LORA), lambda i, j: (0, 0)),
            pl.BlockSpec((bm, LANES), lambda i, j: (i, 0)),
            pl.BlockSpec((bm, LANES), lambda i, j: (i, 0)),
            pl.BlockSpec((Q_LORA, bq), lambda i, j: (0, j)),
            pl.BlockSpec((KV_LORA, bq), lambda i, j: (0, j)),
        ],
        out_specs=[
            pl.BlockSpec((bm, bq), lambda i, j: (i, j)),
            pl.BlockSpec((bm, bq), lambda i, j: (i, j)),
            pl.BlockSpec((bm, heads * V_PAD), lambda i, j: (i, j)),
        ],
        out_shape=[
            jax.ShapeDtypeStruct((s, MLA_HEADS * QK_PAD), BF16),
            jax.ShapeDtypeStruct((s, MLA_HEADS * QK_PAD), BF16),
            jax.ShapeDtypeStruct((s, MLA_HEADS * V_PAD), BF16),
        ],
        scratch_shapes=[
            pltpu.VMEM((bm, Q_LORA), BF16),
            pltpu.VMEM((bm, KV_LORA), BF16),
            pltpu.VMEM((bm, LANES), BF16),
        ],
        compiler_params=_params("parallel", "arbitrary"),
        name="mla_prep",
    )(proj, proj, proj, gq, gkv, cos, sin, wq, wkv)


def _normalize(acc):
    return acc[:, :V_HEAD] / acc[:, V_HEAD:V_HEAD + 1]


def _mla_attn_kernel(q_ref, k_ref, v_ref, wa_ref, wb_ref, o_ref, wc_ref, *, tq, tk, nb):
    t = pl.program_id(0) * pl.num_programs(1) + pl.program_id(1)

    @pl.when(t < nb)
    def _():
        wc_ref[...] = wa_ref[...].astype(wc_ref.dtype)

    @pl.when(t >= nb)
    def _():
        wc_ref[...] = wb_ref[...].astype(wc_ref.dtype)

    for u in range(q_ref.shape[0] // tq):
        q_rows = slice(u * tq, (u + 1) * tq)
        q = q_ref[q_rows, :]
        m = acc = None
        for c in range(k_ref.shape[0] // tk):
            rows = slice(c * tk, (c + 1) * tk)
            s = lax.dot_general(q, k_ref[rows, :], _NT, preferred_element_type=F32)
            m_c = jnp.max(s, axis=-1, keepdims=True)
            if c == 0:
                m = m_c
                acc = jnp.dot(jnp.exp2(s - m).astype(BF16), v_ref[rows, :], preferred_element_type=F32)
            else:
                m_new = jnp.maximum(m, m_c)
                pv = jnp.dot(jnp.exp2(s - m_new).astype(BF16), v_ref[rows, :], preferred_element_type=F32)
                acc = jnp.exp2(m - m_new) * acc + pv
                m = m_new
        o_ref[q_rows, :] = _normalize(acc).astype(o_ref.dtype)


def _mla_attn(q, k, v, wa, wb, tq=512, sub=4, tk=1024):
    s = q.shape[0]
    ni = s // (tq * sub)
    steps = MLA_HEADS * ni
    r, c = wa.shape
    assert wa.shape == wb.shape and (2 * r) % steps == 0
    rb = 2 * r // steps
    nb = r // rb
    kern = functools.partial(_mla_attn_kernel, tq=tq, tk=tk, nb=nb)
    return pl.pallas_call(
        kern,
        grid=(MLA_HEADS, ni),
        in_specs=[
            pl.BlockSpec((tq * sub, QK_PAD), lambda h, i: (i, h)),
            pl.BlockSpec((s, QK_PAD), lambda h, i: (0, h)),
            pl.BlockSpec((s, V_PAD), lambda h, i: (0, h)),
            pl.BlockSpec((rb, c), lambda h, i: (jnp.minimum(h * ni + i, nb - 1), 0)),
            pl.BlockSpec((rb, c), lambda h, i: (jnp.maximum(h * ni + i - nb, 0), 0)),
        ],
        out_specs=[
            pl.BlockSpec((tq * sub, V_HEAD), lambda h, i: (i, h)),
            pl.BlockSpec((rb, c), lambda h, i: (h * ni + i, 0)),
        ],
        out_shape=[
            jax.ShapeDtypeStruct((s, MLA_WIDTH), BF16),
            jax.ShapeDtypeStruct((2 * r, c), BF16),
        ],
        compiler_params=_params("arbitrary", "arbitrary"),
        name="mla_attn",
    )(q, k, v, wa, wb)


DIL_T = 256
DIL_SUB = 16
DIL_REACH = max(w // 2 for w, _ in DIL_PATTERNS)
DIL_SIDE = DIL_REACH // DIL_T
DIL_NCHUNK = 2 * DIL_SIDE + 1


def _dil_bias(r, slope):
    row = lax.broadcasted_iota(jnp.int32, (DIL_T, DIL_T), 0)
    col = lax.broadcasted_iota(jnp.int32, (DIL_T, DIL_T), 1)
    d = (r - DIL_SIDE) * DIL_T + col - row
    ad = jnp.abs(d)
    cnt = jnp.zeros((DIL_T, DIL_T), F32)
    for window, dil in DIL_PATTERNS:
        member = ((d & (dil - 1)) == 0) & (ad <= window // 2)
        cnt = cnt + member.astype(F32)
    bias = jnp.log2(jnp.maximum(cnt, 1.0)) - (slope * LOG2E) * ad.astype(F32)
    return jnp.where(cnt > 0.0, bias, MASK_BIAS)


def _dil_attn_kernel(slope_ref, q_ref, k_ref, v_ref, o_ref, b_ref, vp_ref, *, nblk):
    h = pl.program_id(0)
    i = pl.program_id(1)

    @pl.when(i == 0)
    def _():
        slope = slope_ref[h]
        for r in range(DIL_NCHUNK):
            b_ref[r] = _dil_bias(r, slope)
        b_ref[DIL_NCHUNK] = jnp.full((DIL_T, DIL_T), MASK_BIAS, F32)
        vp_ref[:, :DIL_HEAD_DIM] = v_ref[...]
        vp_ref[:, DIL_HEAD_DIM:] = jnp.ones((vp_ref.shape[0], V_PAD - DIL_HEAD_DIM), BF16)

    for u in range(DIL_SUB):
        rows = slice(u * DIL_T, (u + 1) * DIL_T)
        q = q_ref[rows, :]
        scores, starts = [], []
        for r in range(DIL_NCHUNK):
            kc = i * DIL_SUB + u + (r - DIL_SIDE)
            inside = (kc >= 0) & (kc < nblk)
            ks = pl.multiple_of(jnp.clip(kc, 0, nblk - 1) * DIL_T, DIL_T)
            bias = b_ref[jnp.where(inside, r, DIL_NCHUNK)]
            scores.append(lax.dot_general(q, k_ref[pl.ds(ks, DIL_T), :], _NT, preferred_element_type=F32) + bias)
            starts.append(ks)
        m = functools.reduce(jnp.maximum, [jnp.max(s, axis=-1, keepdims=True) for s in scores])
        acc = None
        for s, ks in zip(scores, starts):
            pv = jnp.dot(jnp.exp2(s - m).astype(BF16), vp_ref[pl.ds(ks, DIL_T), :], preferred_element_type=F32)
            acc = pv if acc is None else acc + pv
        o_ref[rows, :] = _normalize(acc).astype(o_ref.dtype)


def _dil_attn(proj, slopes):
    s = proj.shape[0]
    nblk = s // DIL_T
    kern = functools.partial(_dil_attn_kernel, nblk=nblk)
    hd = DIL_HEAD_DIM
    return pl.pallas_call(
        kern,
        grid=(DIL_HEADS, nblk // DIL_SUB),
        in_specs=[
            pl.BlockSpec(memory_space=pltpu.SMEM),
            pl.BlockSpec((DIL_T * DIL_SUB, hd), lambda h, i: (i, COL_QB // hd + h)),
            pl.BlockSpec((s, hd), lambda h, i: (0, COL_KB // hd + h)),
            pl.BlockSpec((s, hd), lambda h, i: (0, COL_VB // hd + h)),
        ],
        out_specs=pl.BlockSpec((DIL_T * DIL_SUB, hd), lambda h, i: (i, h)),
        out_shape=jax.ShapeDtypeStruct((s, DIL_WIDTH), BF16),
        scratch_shapes=[
            pltpu.VMEM((DIL_NCHUNK + 1, DIL_T, DIL_T), F32),
            pltpu.VMEM((s, V_PAD), BF16),
        ],
        compiler_params=_params("arbitrary", "arbitrary"),
        name="dil_attn",
    )(slopes, proj, proj, proj)


def _gated_norm(a_ref, gate_ref, g_ref, y_ref):
    for r in range(0, a_ref.shape[0], ROW_CHUNK):
        rows = slice(r, r + ROW_CHUNK)
        gate = gate_ref[rows, :].astype(F32)
        y = _rms(a_ref[rows, :].astype(F32), g_ref[...]) * (gate * jax.nn.sigmoid(gate))
        y_ref[rows, :] = y.astype(y_ref.dtype)


def _tail_kernel(aa_ref, ab_ref, ga_ref, gb_ref, gma_ref, gmb_ref, x_ref, w_ref, gple_ref, p_ref, wp_ref,
                 gfin_ref, o_ref, yh_ref, *, nj, bn, last_layer):
    jj = pl.program_id(1)

    @pl.when(jj == 0)
    def _():
        _gated_norm(aa_ref, ga_ref, gma_ref, yh_ref.at[:, :MLA_WIDTH])
        _gated_norm(ab_ref, gb_ref, gmb_ref, yh_ref.at[:, MLA_WIDTH:])

    @pl.when(jj < nj)
    def _():
        cols = pl.ds(pl.multiple_of(jj * bn, bn), bn)
        o_ref[:, cols] = x_ref[...] + jnp.dot(yh_ref[...], w_ref[...], preferred_element_type=F32)

    @pl.when(jj == nj)
    def _():
        _rms_rows(o_ref, gple_ref, yh_ref)

    @pl.when(jj >= nj)
    def _():
        cols = pl.ds(pl.multiple_of((jj - nj) * bn, bn), bn)
        gate = jax.nn.sigmoid(jnp.dot(yh_ref[...], w_ref[...], preferred_element_type=F32))
        ple = jnp.dot(p_ref[...].astype(BF16), wp_ref[...], preferred_element_type=F32)
        o_ref[:, cols] = o_ref[:, cols] + ple * gate

    if last_layer:
        @pl.when(jj == 2 * nj - 1)
        def _():
            _rms_rows(o_ref, gfin_ref, o_ref)


def _tail(attn_a, attn_b, proj, g_a, g_b, x, w_cat, g_ple, p, w_ple, g_fin, last_layer, bm=512, bn=512):
    s, d = x.shape
    assert d == MLA_WIDTH + DIL_WIDTH
    c = p.shape[1]
    nj = d // bn
    kern = functools.partial(_tail_kernel, nj=nj, bn=bn, last_layer=last_layer)
    first = lambda i, jj: (i, jnp.minimum(jj, nj - 1))
    second = lambda i, jj: (0, jnp.maximum(jj - nj, 0))
    row = lambda i, jj: (i, 0)
    vec = lambda i, jj: (0, 0)
    return pl.pallas_call(
        kern,
        grid=(s // bm, 2 * nj),
        in_specs=[
            pl.BlockSpec((bm, MLA_WIDTH), row),
            pl.BlockSpec((bm, DIL_WIDTH), row),
            pl.BlockSpec((bm, MLA_WIDTH), lambda i, jj: (i, COL_GATE_A // MLA_WIDTH)),
            pl.BlockSpec((bm, DIL_WIDTH), lambda i, jj: (i, COL_GATE_B // DIL_WIDTH)),
            pl.BlockSpec((1, MLA_WIDTH), vec),
            pl.BlockSpec((1, DIL_WIDTH), vec),
            pl.BlockSpec((bm, bn), first),
            pl.BlockSpec((d, bn), lambda i, jj: (jj // nj, jj % nj)),
            pl.BlockSpec((1, d), vec),
            pl.BlockSpec((bm, c), row),
            pl.BlockSpec((c, bn), second),
            pl.BlockSpec((1, d), vec),
        ],
        out_specs=pl.BlockSpec((bm, d), row),
        out_shape=jax.ShapeDtypeStruct((s, d), F32),
        scratch_shapes=[pltpu.VMEM((bm, d), BF16)],
        compiler_params=_params("parallel", "arbitrary"),
        name="tail",
    )(attn_a, attn_b, proj, proj, g_a, g_b, x, w_cat, g_ple, p, w_ple, g_fin)


def _prep_w_in(w):
    col_scale = np.ones((1, PROJ_WIDTH), np.float32)
    col_scale[0, COL_QB:COL_KB] = DIL_HEAD_DIM ** -0.5 * LOG2E
    return w.T, jnp.asarray(col_scale)


def _prep_w_uq(w):
    r = w.shape[0]
    w = w.reshape(r, MLA_HEADS, QK_NOPE + QK_ROPE) * ((QK_NOPE + QK_ROPE) ** -0.5 * LOG2E)
    w = jnp.pad(w, ((0, 0), (0, 0), (0, QK_PAD - QK_NOPE - QK_ROPE)))
    return w.reshape(r, MLA_HEADS * QK_PAD).astype(BF16)


def _rope_tables(positions):
    inv = ROPE_THETA ** (-jnp.arange(ROPE_HALF, dtype=F32) / ROPE_HALF)
    ang = positions.astype(F32)[:, None] * inv
    cos, sin = jnp.cos(ang), jnp.sin(ang)
    zero = jnp.zeros_like(cos)
    return (jnp.concatenate([cos, cos, zero, zero], axis=1),
            jnp.concatenate([-sin, sin, zero, zero], axis=1))


def kernel(x, p, positions, g_mix, w_in, g_q_latent, w_uq, g_kv_latent, w_ukv, g_out_mla, g_out_dil,
           w_out, w_ple, g_ple, w_ple_gate, g_final):
    b, s, d = x.shape
    assert b == 1 and s % (DIL_T * DIL_SUB) == 0
    slopes = jnp.asarray(2.0 ** (-ALIBI_MAX_BIAS * np.arange(1, DIL_HEADS + 1) / DIL_HEADS), F32)
    x2d = x.reshape(s, d)
    cos, sin = _rope_tables(positions[0])
    depth = g_mix.shape[0]
    for i in range(depth):
        proj = _in_proj(x2d, g_mix[i][None], *_prep_w_in(w_in[i]))
        q_a, k_a, v_a = _mla_prep(proj, g_q_latent[i][None], g_kv_latent[i][None], cos, sin,
                                  _prep_w_uq(w_uq[i]), w_ukv[i].astype(BF16))
        attn_a, w_cat = _mla_attn(q_a, k_a, v_a, w_out[i], w_ple_gate[i])
        attn_b = _dil_attn(proj, slopes)
        x2d = _tail(attn_a, attn_b, proj, g_out_mla[i][None], g_out_dil[i][None], x2d, w_cat,
                    g_ple[i][None], p[i, 0], w_ple[i].astype(BF16), g_final[None], i == depth - 1)
    return x2d.reshape(b, s, d)
```

```python
import functools

import numpy as np
import jax
import jax.numpy as jnp
from jax import lax
from jax.experimental import pallas as pl
from jax.experimental.pallas import tpu as pltpu

F32 = jnp.float32
BF16 = jnp.bfloat16

EPS = 1e-6
ROPE_THETA = 10000.0
ALIBI_MAX_BIAS = 8.0

MLA_HEADS = 16
Q_LORA = 1024
KV_LORA = 512
QK_NOPE = 128
QK_ROPE = 64
V_HEAD = 128
MLA_WIDTH = MLA_HEADS * V_HEAD
DIL_HEADS = 16
DIL_HEAD_DIM = 128
DIL_WIDTH = DIL_HEADS * DIL_HEAD_DIM
DIL_PATTERNS = ((128, 1), (512, 4), (2048, 16))

LANES = 128
QK_PAD = 256
V_PAD = 2 * LANES
ROPE_HALF = QK_ROPE // 2
LOG2E = float(np.log2(np.e))

COL_GATE_A = 0
COL_QB = COL_GATE_A + MLA_WIDTH
COL_KB = COL_QB + DIL_WIDTH
COL_VB = COL_KB + DIL_WIDTH
COL_GATE_B = COL_VB + DIL_WIDTH
COL_CQ = COL_GATE_B + DIL_WIDTH
COL_CKV = COL_CQ + Q_LORA
COL_KR = COL_CKV + KV_LORA
PROJ_WIDTH = 12288

MASK_BIAS = -1e30

ROW_CHUNK = 64

VMEM_LIMIT = 60 * 1024 * 1024


def _params(*sem):
    return pltpu.CompilerParams(dimension_semantics=sem, vmem_limit_bytes=VMEM_LIMIT)


def _rms(x, g):
    return x * lax.rsqrt(jnp.mean(x * x, axis=-1, keepdims=True) + EPS) * g


def _rms_rows(x_ref, g_ref, h_ref):
    for r in range(0, x_ref.shape[0], ROW_CHUNK):
        rows = slice(r, r + ROW_CHUNK)
        h_ref[rows, :] = _rms(x_ref[rows, :].astype(F32), g_ref[...]).astype(h_ref.dtype)


_NT = (((1,), (1,)), ((), ()))


def _in_proj_kernel(x_ref, g_ref, w_ref, cs_ref, o_ref, h_ref, *, parts):
    j = pl.program_id(1)
    bx = x_ref.shape[0]

    for part in range(parts):
        @pl.when(j == part)
        def _():
            _rms_rows(x_ref, g_ref, h_ref.at[part * bx:(part + 1) * bx, :])

    @pl.when(j >= parts)
    def _():
        acc = lax.dot_general(h_ref[...], w_ref[...].astype(BF16), _NT, preferred_element_type=F32)
        o_ref[...] = (acc * cs_ref[...]).astype(o_ref.dtype)


def _in_proj(x, g, wt, col_scale, bx=512, parts=4, bn=512):
    s, d = x.shape
    bm = bx * parts
    feat_main = Q_LORA + KV_LORA + QK_ROPE
    n_main = (wt.shape[0] - feat_main) // bn
    n_out = PROJ_WIDTH // bn
    assert n_main * bn == COL_CQ and feat_main % 64 == 0 and (n_out - n_main) * bn <= wt.shape[0]

    def w_index(i, j):
        n = jnp.maximum(j - parts, 0)
        row = jnp.where(n < n_main, feat_main + n * bn, (n - n_main) * bn)
        return pl.multiple_of(row, 64), 0

    return pl.pallas_call(
        functools.partial(_in_proj_kernel, parts=parts),
        grid=(s // bm, parts + n_out),
        in_specs=[
            pl.BlockSpec((bx, d), lambda i, j: (i * parts + jnp.minimum(j, parts - 1), 0)),
            pl.BlockSpec((1, d), lambda i, j: (0, 0)),
            pl.BlockSpec((pl.Element(bn), pl.Element(d)), w_index),
            pl.BlockSpec((1, bn), lambda i, j: (0, jnp.maximum(j - parts, 0))),
        ],
        out_specs=pl.BlockSpec((bm, bn), lambda i, j: (i, jnp.maximum(j - parts, 0))),
        out_shape=jax.ShapeDtypeStruct((s, PROJ_WIDTH), BF16),
        scratch_shapes=[pltpu.VMEM((bm, d), BF16)],
        compiler_params=_params("parallel", "arbitrary"),
        name="in_proj",
    )(x, g, wt, col_scale)


def _rope(t, cos, sin):
    lane = lax.broadcasted_iota(jnp.int32, t.shape, 1)
    swapped = jnp.where(lane < ROPE_HALF, pltpu.roll(t, LANES - ROPE_HALF, axis=1), pltpu.roll(t, ROPE_HALF, axis=1))
    return t * cos + swapped * sin


def _mla_prep_kernel(cq_ref, ckv_ref, kr_ref, gq_ref, gkv_ref, cos_ref, sin_ref, wq_ref, wkv_ref,
                     q_out, k_out, v_out, cqn_ref, ckvn_ref, kpe_ref, *, heads):
    @pl.when(pl.program_id(1) == 0)
    def _():
        cqn_ref[...] = _rms(cq_ref[...].astype(F32), gq_ref[...]).astype(BF16)
        ckvn_ref[...] = _rms(ckv_ref[...].astype(F32), gkv_ref[...]).astype(BF16)
        kpe_ref[...] = _rope(kr_ref[...].astype(F32), cos_ref[...], sin_ref[...]).astype(BF16)

    q = jnp.dot(cqn_ref[...], wq_ref[...], preferred_element_type=F32)
    kv = jnp.dot(ckvn_ref[...], wkv_ref[...], preferred_element_type=F32)
    cos = cos_ref[...]
    sin = sin_ref[...]
    for g in range(heads):
        lo = g * QK_PAD
        mid = lo + LANES
        hi = lo + QK_PAD
        q_out[:, lo:mid] = q[:, lo:mid].astype(BF16)
        q_out[:, mid:hi] = _rope(q[:, mid:hi], cos, sin).astype(BF16)
        k_out[:, lo:mid] = kv[:, lo:mid].astype(BF16)
        k_out[:, mid:hi] = kpe_ref[...]
        v_out[:, lo:mid] = kv[:, mid:hi].astype(BF16)
        v_out[:, mid:hi] = jnp.ones((v_out.shape[0], LANES), BF16)


def _mla_prep(proj, gq, gkv, cos, sin, wq, wkv, bm=256, heads=MLA_HEADS):
    s = proj.shape[0]
    bq = heads * QK_PAD
    kern = functools.partial(_mla_prep_kernel, heads=heads)
    return pl.pallas_call(
        kern,
        grid=(s // bm, MLA_HEADS // heads),
        in_specs=[
            pl.BlockSpec((bm, Q_LORA), lambda i, j: (i, COL_CQ // Q_LORA)),
            pl.BlockSpec((bm, KV_LORA), lambda i, j: (i, COL_CKV // KV_LORA)),
            pl.BlockSpec((bm, LANES), lambda i, j: (i, COL_KR // LANES)),
            pl.BlockSpec((1, Q_LORA), lambda i, j: (0, 0)),
            pl.BlockSpec((1, KV_LORA), lambda i, j: (0, 0)),
            pl.BlockSpec((bm, LANES), lambda i, j: (i, 0)),
            pl.BlockSpec((bm, LANES), lambda i, j: (i, 0)),
            pl.BlockSpec((Q_LORA, bq), lambda i, j: (0, j)),
            pl.BlockSpec((KV_LORA, bq), lambda i, j: (0, j)),
        ],
        out_specs=[
            pl.BlockSpec((bm, bq), lambda i, j: (i, j)),
            pl.BlockSpec((bm, bq), lambda i, j: (i, j)),
            pl.BlockSpec((bm, heads * V_PAD), lambda i, j: (i, j)),
        ],
        out_shape=[
            jax.ShapeDtypeStruct((s, MLA_HEADS * QK_PAD), BF16),
            jax.ShapeDtypeStruct((s, MLA_HEADS * QK_PAD), BF16),
            jax.ShapeDtypeStruct((s, MLA_HEADS * V_PAD), BF16),
        ],
        scratch_shapes=[
            pltpu.VMEM((bm, Q_LORA), BF16),
            pltpu.VMEM((bm, KV_LORA), BF16),
            pltpu.VMEM((bm, LANES), BF16),
        ],
        compiler_params=_params("parallel", "arbitrary"),
        name="mla_prep",
    )(proj, proj, proj, gq, gkv, cos, sin, wq, wkv)


def _normalize(acc):
    return acc[:, :V_HEAD] / acc[:, V_HEAD:V_HEAD + 1]


def _mla_attn_kernel(q_ref, k_ref, v_ref, wa_ref, wb_ref, o_ref, wc_ref, *, tq, tk, nb):
    t = pl.program_id(0) * pl.num_programs(1) + pl.program_id(1)

    @pl.when(t < nb)
    def _():
        wc_ref[...] = wa_ref[...].astype(wc_ref.dtype)

    @pl.when(t >= nb)
    def _():
        wc_ref[...] = wb_ref[...].astype(wc_ref.dtype)

    for u in range(q_ref.shape[0] // tq):
        q_rows = slice(u * tq, (u + 1) * tq)
        q = q_ref[q_rows, :]
        m = acc = None
        for c in range(k_ref.shape[0] // tk):
            rows = slice(c * tk, (c + 1) * tk)
            s = lax.dot_general(q, k_ref[rows, :], _NT, preferred_element_type=F32)
            m_c = jnp.max(s, axis=-1, keepdims=True)
            if c == 0:
                m = m_c
                acc = jnp.dot(jnp.exp2(s - m).astype(BF16), v_ref[rows, :], preferred_element_type=F32)
            else:
                m_new = jnp.maximum(m, m_c)
                pv = jnp.dot(jnp.exp2(s - m_new).astype(BF16), v_ref[rows, :], preferred_element_type=F32)
                acc = jnp.exp2(m - m_new) * acc + pv
                m = m_new
        o_ref[q_rows, :] = _normalize(acc).astype(o_ref.dtype)


def _mla_attn(q, k, v, wa, wb, tq=512, sub=4, tk=1024):
    s = q.shape[0]
    ni = s // (tq * sub)
    steps = MLA_HEADS * ni
    r, c = wa.shape
    assert wa.shape == wb.shape and (2 * r) % steps == 0
    rb = 2 * r // steps
    nb = r // rb
    kern = functools.partial(_mla_attn_kernel, tq=tq, tk=tk, nb=nb)
    return pl.pallas_call(
        kern,
        grid=(MLA_HEADS, ni),
        in_specs=[
            pl.BlockSpec((tq * sub, QK_PAD), lambda h, i: (i, h)),
            pl.BlockSpec((s, QK_PAD), lambda h, i: (0, h)),
            pl.BlockSpec((s, V_PAD), lambda h, i: (0, h)),
            pl.BlockSpec((rb, c), lambda h, i: (jnp.minimum(h * ni + i, nb - 1), 0)),
            pl.BlockSpec((rb, c), lambda h, i: (jnp.maximum(h * ni + i - nb, 0), 0)),
        ],
        out_specs=[
            pl.BlockSpec((tq * sub, V_HEAD), lambda h, i: (i, h)),
            pl.BlockSpec((rb, c), lambda h, i: (h * ni + i, 0)),
        ],
        out_shape=[
            jax.ShapeDtypeStruct((s, MLA_WIDTH), BF16),
            jax.ShapeDtypeStruct((2 * r, c), BF16),
        ],
        compiler_params=_params("arbitrary", "arbitrary"),
        name="mla_attn",
    )(q, k, v, wa, wb)


DIL_T = 256
DIL_SUB = 16
DIL_REACH = max(w // 2 for w, _ in DIL_PATTERNS)
DIL_SIDE = DIL_REACH // DIL_T
DIL_NCHUNK = 2 * DIL_SIDE + 1


def _dil_tables(r):
    row = lax.broadcasted_iota(jnp.int32, (DIL_T, DIL_T), 0)
    col = lax.broadcasted_iota(jnp.int32, (DIL_T, DIL_T), 1)
    d = (r - DIL_SIDE) * DIL_T + col - row
    ad = jnp.abs(d)
    cnt = jnp.zeros((DIL_T, DIL_T), F32)
    for window, dil in DIL_PATTERNS:
        member = ((d & (dil - 1)) == 0) & (ad <= window // 2)
        cnt = cnt + member.astype(F32)
    attended = cnt > 0.0
    return (jnp.where(attended, jnp.log2(jnp.maximum(cnt, 1.0)), MASK_BIAS),
            jnp.where(attended, ad.astype(F32), 0.0))


def _dil_attn_kernel(slope_ref, q_ref, k_ref, v_ref, o_ref, b_ref, vp_ref, logc_ref, dist_ref, *, nblk):
    h = pl.program_id(0)
    i = pl.program_id(1)

    @pl.when((h == 0) & (i == 0))
    def _():
        for r in range(DIL_NCHUNK):
            logc_ref[r], dist_ref[r] = _dil_tables(r)
        b_ref[DIL_NCHUNK] = jnp.full((DIL_T, DIL_T), MASK_BIAS, F32)

    @pl.when(i == 0)
    def _():
        slope2 = slope_ref[h] * LOG2E
        for r in range(DIL_NCHUNK):
            b_ref[r] = logc_ref[r] - slope2 * dist_ref[r]
        vp_ref[:, :DIL_HEAD_DIM] = v_ref[...]
        vp_ref[:, DIL_HEAD_DIM:] = jnp.ones((vp_ref.shape[0], V_PAD - DIL_HEAD_DIM), BF16)

    for u in range(DIL_SUB):
        rows = slice(u * DIL_T, (u + 1) * DIL_T)
        q = q_ref[rows, :]
        scores, starts = [], []
        for r in range(DIL_NCHUNK):
            kc = i * DIL_SUB + u + (r - DIL_SIDE)
            inside = (kc >= 0) & (kc < nblk)
            ks = pl.multiple_of(jnp.clip(kc, 0, nblk - 1) * DIL_T, DIL_T)
            bias = b_ref[jnp.where(inside, r, DIL_NCHUNK)]
            scores.append(lax.dot_general(q, k_ref[pl.ds(ks, DIL_T), :], _NT, preferred_element_type=F32) + bias)
            starts.append(ks)
        m = functools.reduce(jnp.maximum, [jnp.max(s, axis=-1, keepdims=True) for s in scores])
        acc = None
        for s, ks in zip(scores, starts):
            pv = jnp.dot(jnp.exp2(s - m).astype(BF16), vp_ref[pl.ds(ks, DIL_T), :], preferred_element_type=F32)
            acc = pv if acc is None else acc + pv
        o_ref[rows, :] = _normalize(acc).astype(o_ref.dtype)


def _dil_attn(proj, slopes):
    s = proj.shape[0]
    nblk = s // DIL_T
    kern = functools.partial(_dil_attn_kernel, nblk=nblk)
    hd = DIL_HEAD_DIM
    return pl.pallas_call(
        kern,
        grid=(DIL_HEADS, nblk // DIL_SUB),
        in_specs=[
            pl.BlockSpec(memory_space=pltpu.SMEM),
            pl.BlockSpec((DIL_T * DIL_SUB, hd), lambda h, i: (i, COL_QB // hd + h)),
            pl.BlockSpec((s, hd), lambda h, i: (0, COL_KB // hd + h)),
            pl.BlockSpec((s, hd), lambda h, i: (0, COL_VB // hd + h)),
        ],
        out_specs=pl.BlockSpec((DIL_T * DIL_SUB, hd), lambda h, i: (i, h)),
        out_shape=jax.ShapeDtypeStruct((s, DIL_WIDTH), BF16),
        scratch_shapes=[
            pltpu.VMEM((DIL_NCHUNK + 1, DIL_T, DIL_T), F32),
            pltpu.VMEM((s, V_PAD), BF16),
            pltpu.VMEM((DIL_NCHUNK, DIL_T, DIL_T), F32),
            pltpu.VMEM((DIL_NCHUNK, DIL_T, DIL_T), F32),
        ],
        compiler_params=_params("arbitrary", "arbitrary"),
        name="dil_attn",
    )(slopes, proj, proj, proj)


def _gated_norm(a_ref, gate_ref, g_ref, y_ref):
    for r in range(0, a_ref.shape[0], ROW_CHUNK):
        rows = slice(r, r + ROW_CHUNK)
        gate = gate_ref[rows, :].astype(F32)
        y = _rms(a_ref[rows, :].astype(F32), g_ref[...]) * (gate * jax.nn.sigmoid(gate))
        y_ref[rows, :] = y.astype(y_ref.dtype)


def _tail_kernel(aa_ref, ab_ref, ga_ref, gb_ref, gma_ref, gmb_ref, x_ref, w_ref, gple_ref, p_ref, wp_ref,
                 gfin_ref, o_ref, yh_ref, *, nj, bn, last_layer):
    jj = pl.program_id(1)

    @pl.when(jj == 0)
    def _():
        _gated_norm(aa_ref, ga_ref, gma_ref, yh_ref.at[:, :MLA_WIDTH])
        _gated_norm(ab_ref, gb_ref, gmb_ref, yh_ref.at[:, MLA_WIDTH:])

    @pl.when(jj < nj)
    def _():
        cols = pl.ds(pl.multiple_of(jj * bn, bn), bn)
        o_ref[:, cols] = x_ref[...] + jnp.dot(yh_ref[...], w_ref[...], preferred_element_type=F32)

    @pl.when(jj == nj)
    def _():
        _rms_rows(o_ref, gple_ref, yh_ref)

    @pl.when(jj >= nj)
    def _():
        cols = pl.ds(pl.multiple_of((jj - nj) * bn, bn), bn)
        gate = jax.nn.sigmoid(jnp.dot(yh_ref[...], w_ref[...], preferred_element_type=F32))
        ple = jnp.dot(p_ref[...].astype(BF16), wp_ref[...], preferred_element_type=F32)
        o_ref[:, cols] = o_ref[:, cols] + ple * gate

    if last_layer:
        @pl.when(jj == 2 * nj - 1)
        def _():
            _rms_rows(o_ref, gfin_ref, o_ref)


def _tail(attn_a, attn_b, proj, g_a, g_b, x, w_cat, g_ple, p, w_ple, g_fin, last_layer, bm=512, bn=512):
    s, d = x.shape
    assert d == MLA_WIDTH + DIL_WIDTH
    c = p.shape[1]
    nj = d // bn
    kern = functools.partial(_tail_kernel, nj=nj, bn=bn, last_layer=last_layer)
    first = lambda i, jj: (i, jnp.minimum(jj, nj - 1))
    second = lambda i, jj: (0, jnp.maximum(jj - nj, 0))
    row = lambda i, jj: (i, 0)
    vec = lambda i, jj: (0, 0)
    return pl.pallas_call(
        kern,
        grid=(s // bm, 2 * nj),
        in_specs=[
            pl.BlockSpec((bm, MLA_WIDTH), row),
            pl.BlockSpec((bm, DIL_WIDTH), row),
            pl.BlockSpec((bm, MLA_WIDTH), lambda i, jj: (i, COL_GATE_A // MLA_WIDTH)),
            pl.BlockSpec((bm, DIL_WIDTH), lambda i, jj: (i, COL_GATE_B // DIL_WIDTH)),
            pl.BlockSpec((1, MLA_WIDTH), vec),
            pl.BlockSpec((1, DIL_WIDTH), vec),
            pl.BlockSpec((bm, bn), first),
            pl.BlockSpec((d, bn), lambda i, jj: (jj // nj, jj % nj)),
            pl.BlockSpec((1, d), vec),
            pl.BlockSpec((bm, c), row),
            pl.BlockSpec((c, bn), second),
            pl.BlockSpec((1, d), vec),
        ],
        out_specs=pl.BlockSpec((bm, d), row),
        out_shape=jax.ShapeDtypeStruct((s, d), F32),
        scratch_shapes=[pltpu.VMEM((bm, d), BF16)],
        compiler_params=_params("parallel", "arbitrary"),
        name="tail",
    )(attn_a, attn_b, proj, proj, g_a, g_b, x, w_cat, g_ple, p, w_ple, g_fin)


def _prep_w_in(w):
    col_scale = np.ones((1, PROJ_WIDTH), np.float32)
    col_scale[0, COL_QB:COL_KB] = DIL_HEAD_DIM ** -0.5 * LOG2E
    return w.T, jnp.asarray(col_scale)


def _prep_w_uq(w):
    r = w.shape[0]
    w = w.reshape(r, MLA_HEADS, QK_NOPE + QK_ROPE) * ((QK_NOPE + QK_ROPE) ** -0.5 * LOG2E)
    w = jnp.pad(w, ((0, 0), (0, 0), (0, QK_PAD - QK_NOPE - QK_ROPE)))
    return w.reshape(r, MLA_HEADS * QK_PAD).astype(BF16)


def _rope_tables(positions):
    inv = ROPE_THETA ** (-jnp.arange(ROPE_HALF, dtype=F32) / ROPE_HALF)
    ang = positions.astype(F32)[:, None] * inv
    cos, sin = jnp.cos(ang), jnp.sin(ang)
    zero = jnp.zeros_like(cos)
    return (jnp.concatenate([cos, cos, zero, zero], axis=1),
            jnp.concatenate([-sin, sin, zero, zero], axis=1))


def kernel(x, p, positions, g_mix, w_in, g_q_latent, w_uq, g_kv_latent, w_ukv, g_out_mla, g_out_dil,
           w_out, w_ple, g_ple, w_ple_gate, g_final):
    b, s, d = x.shape
    assert b == 1 and s % (DIL_T * DIL_SUB) == 0
    slopes = jnp.asarray(2.0 ** (-ALIBI_MAX_BIAS * np.arange(1, DIL_HEADS + 1) / DIL_HEADS), F32)
    x2d = x.reshape(s, d)
    cos, sin = _rope_tables(positions[0])
    depth = g_mix.shape[0]
    for i in range(depth):
        proj = _in_proj(x2d, g_mix[i][None], *_prep_w_in(w_in[i]))
        q_a, k_a, v_a = _mla_prep(proj, g_q_latent[i][None], g_kv_latent[i][None], cos, sin,
                                  _prep_w_uq(w_uq[i]), w_ukv[i].astype(BF16))
        attn_a, w_cat = _mla_attn(q_a, k_a, v_a, w_out[i], w_ple_gate[i])
        attn_b = _dil_attn(proj, slopes)
        x2d = _tail(attn_a, attn_b, proj, g_out_mla[i][None], g_out_dil[i][None], x2d, w_cat,
                    g_ple[i][None], p[i, 0], w_ple[i].astype(BF16), g_final[None], i == depth - 1)
    return x2d.reshape(b, s, d)
```
